```python
import jax, jax.numpy as jnp
from jax import lax
import numpy as np

D_MODEL = 4096
BATCH = 2
SEQ = 8192
DEPTH = 2

CHUNK = 64
EPS = 1e-6

A_HEADS = 8
A_QK = 128
A_V = 256
A_CONV = 4
A_QK_W = A_HEADS * A_QK
A_V_W = A_HEADS * A_V

B_HEADS = 16
B_NOPE = 128
B_ROPE = 64
B_V = 128
Q_LORA = 1024
KV_LORA = 512
ROPE_BASE = 10000.0
Q_BLOCK = 128
B_Q_W = B_HEADS * (B_NOPE + B_ROPE)
B_KV_W = B_HEADS * (B_NOPE + B_V)
B_OUT_W = B_HEADS * B_V

D_FF = 14336
N_EXPERTS = 8
TOP_K = 2
D_FF_EXPERT = 3584
MOE_BLOCK = 512
N_DENSE = (DEPTH + 1) // 2
N_MOE = DEPTH // 2

IN_SPLITS = (A_QK_W, A_QK_W, A_V_W, A_V_W, A_HEADS, A_HEADS, Q_LORA, KV_LORA, B_ROPE, D_MODEL, D_MODEL)
IN_COLS = sum(IN_SPLITS)

kernel_name = 'hybrid_mlstm_mla_moe_encoder'


def rmsnorm(x, g):
    xf = x.astype(jnp.float32)
    r = lax.rsqrt(jnp.mean(xf * xf, axis=-1, keepdims=True) + EPS)
    return (xf * r * g.astype(jnp.float32)).astype(x.dtype)


def split_columns(z, sizes):
    outs, off = [], 0
    for s in sizes:
        outs.append(z[..., off:off + s])
        off += s
    return outs


def rope_tables(positions):
    inv = ROPE_BASE ** (-jnp.arange(0, B_ROPE, 2, dtype=jnp.float32) / B_ROPE)
    ang = positions.astype(jnp.float32)[..., None] * inv
    return jnp.cos(ang), jnp.sin(ang)


def apply_rope(x, cos, sin):
    x1, x2 = jnp.split(x.astype(jnp.float32), 2, axis=-1)
    return jnp.concatenate([x1 * cos - x2 * sin, x2 * cos + x1 * sin], axis=-1).astype(x.dtype)


def causal_dwconv(u, w, b):
    K, C = w.shape
    y = lax.conv_general_dilated(u, w[:, None, :].astype(u.dtype), window_strides=(1,),
                                 padding=[(K - 1, 0)], dimension_numbers=('NWC', 'WIO', 'NWC'),
                                 feature_group_count=C)
    return y + b


def to_chunks(t):
    Bn, S, H = t.shape[:3]
    t = t.reshape((Bn, S // CHUNK, CHUNK, H) + t.shape[3:])
    return jnp.swapaxes(t, 2, 3)


def mlstm_chunkwise(q, k, v, i_pre, f_pre):
    Bn, S, H, dk = q.shape
    dv = v.shape[-1]
    f32 = jnp.float32
    qc = to_chunks(q.astype(f32))
    kc = to_chunks(k.astype(f32)) * (dk ** -0.5)
    vc = to_chunks(v.astype(f32))
    ig = to_chunks(i_pre.astype(f32))
    logf = jax.nn.log_sigmoid(to_chunks(f_pre.astype(f32)))
    b = jnp.cumsum(logf, axis=-1)
    b_last = b[..., -1]
    a = b_last[..., None] - b + ig
    m_loc = jnp.max(a, axis=-1)
    wa = jnp.exp(a - m_loc[..., None])
    C_loc = jnp.einsum('bchl,bchlv,bchlk->bchvk', wa, vc, kc)
    n_loc = jnp.einsum('bchl,bchlk->bchk', wa, kc)

    def step(carry, inp):
        C, n, m = carry
        Cl, nl, ml, bl = inp
        m_new = jnp.maximum(bl + m, ml)
        s_old = jnp.exp(bl + m - m_new)
        s_loc = jnp.exp(ml - m_new)
        C_new = s_old[..., None, None] * C + s_loc[..., None, None] * Cl
        n_new = s_old[..., None] * n + s_loc[..., None] * nl
        return (C_new, n_new, m_new), (C, n, m)

    init = (jnp.zeros((Bn, H, dv, dk), f32), jnp.zeros((Bn, H, dk), f32), jnp.zeros((Bn, H), f32))
    xs = (jnp.moveaxis(C_loc, 1, 0), jnp.moveaxis(n_loc, 1, 0), jnp.moveaxis(m_loc, 1, 0), jnp.moveaxis(b_last, 1, 0))
    _, (C_st, n_st, m_st) = lax.scan(step, init, xs)
    C_st = jnp.moveaxis(C_st, 0, 1)
    n_st = jnp.moveaxis(n_st, 0, 1)
    m_st = jnp.moveaxis(m_st, 0, 1)

    causal = jnp.tril(jnp.ones((CHUNK, CHUNK), dtype=bool))
    Dm = jnp.where(causal, b[..., :, None] - b[..., None, :] + ig[..., None, :], -jnp.inf)
    m_inter = b + m_st[..., None]
    m_j = jnp.maximum(m_inter, jnp.max(Dm, axis=-1))
    qk = jnp.einsum('bchjd,bchsd->bchjs', qc, kc) * jnp.exp(Dm - m_j[..., None])
    s_inter = jnp.exp(m_inter - m_j)
    num = jnp.einsum('bchjs,bchsv->bchjv', qk, vc) + s_inter[..., None] * jnp.einsum('bchjd,bchvd->bchjv', qc, C_st)
    den = jnp.sum(qk, axis=-1) + s_inter * jnp.einsum('bchjd,bchd->bchj', qc, n_st)
    h = num / jnp.maximum(jnp.abs(den), jnp.exp(-m_j))[..., None]
    return jnp.swapaxes(h, 2, 3).reshape(Bn, S, H, dv)


def mla_attention(q_nope, q_rope, k_nope, k_rope, v):
    Bn, S, H, _ = q_nope.shape
    scale = (B_NOPE + B_ROPE) ** -0.5
    key_chunk = jnp.arange(S) // CHUNK

    def block(i):
        qs = i * Q_BLOCK
        qn = lax.dynamic_slice_in_dim(q_nope, qs, Q_BLOCK, axis=1)
        qr = lax.dynamic_slice_in_dim(q_rope, qs, Q_BLOCK, axis=1)
        s = (jnp.einsum('bqhn,bkhn->bhqk', qn, k_nope) + jnp.einsum('bqhr,bkr->bhqk', qr, k_rope)).astype(jnp.float32) * scale
        q_chunk = (qs + jnp.arange(Q_BLOCK)) // CHUNK
        s = jnp.where(key_chunk[None, :] <= q_chunk[:, None], s, -jnp.inf)
        p = jax.nn.softmax(s, axis=-1).astype(v.dtype)
        return jnp.einsum('bhqk,bkhv->bqhv', p, v)

    out = lax.map(block, jnp.arange(S // Q_BLOCK))
    return jnp.moveaxis(out, 0, 1).reshape(Bn, S, H * v.shape[-1])


def hybrid_mixer(h, cos, sin, w_in, b_igate, b_fgate, conv_w, conv_b, mlstm_norm, w_branch_a,
                 q_norm, kv_norm, w_uq, w_ukv, w_branch_b, w_out):
    Bn, S, _ = h.shape
    z = h @ w_in
    a_q, a_k, a_v, a_o, a_i, a_f, c_q, c_kv, k_rope, g_a, g_b = split_columns(z, IN_SPLITS)
    qk = jax.nn.silu(causal_dwconv(jnp.concatenate([a_q, a_k], axis=-1), conv_w, conv_b))
    a_q, a_k = jnp.split(qk, 2, axis=-1)
    h_a = mlstm_chunkwise(a_q.reshape(Bn, S, A_HEADS, A_QK), a_k.reshape(Bn, S, A_HEADS, A_QK),
                          a_v.reshape(Bn, S, A_HEADS, A_V), a_i + b_igate, a_f + b_fgate)
    h_a = rmsnorm(h_a, mlstm_norm).reshape(Bn, S, A_V_W).astype(h.dtype)
    y_a = (jax.nn.sigmoid(a_o) * h_a) @ w_branch_a
    q = (rmsnorm(c_q, q_norm) @ w_uq).reshape(Bn, S, B_HEADS, B_NOPE + B_ROPE)
    q_nope, q_rope = q[..., :B_NOPE], q[..., B_NOPE:]
    q_rope = apply_rope(q_rope, cos[:, :, None, :], sin[:, :, None, :])
    kv = (rmsnorm(c_kv, kv_norm) @ w_ukv).reshape(Bn, S, B_HEADS, B_NOPE + B_V)
    k_nope, v = kv[..., :B_NOPE], kv[..., B_NOPE:]
    k_rope = apply_rope(k_rope, cos, sin)
    y_b = mla_attention(q_nope, q_rope, k_nope, k_rope, v) @ w_branch_b
    merged = jax.nn.sigmoid(g_a) * y_a + jax.nn.sigmoid(g_b) * y_b
    return merged @ w_out


def swiglu(h, w_gate, w_up, w_down):
    return (jax.nn.silu(h @ w_gate) * (h @ w_up)) @ w_down


def moe_swiglu(h, w_router, w_gate, w_up, w_down):
    Bn, S, D = h.shape
    N = Bn * S
    xt = h.reshape(N, D)
    logits = (xt @ w_router).astype(jnp.float32)
    top_val, top_idx = lax.top_k(logits, TOP_K)
    gates = jax.nn.softmax(top_val, axis=-1)
    A = N * TOP_K
    e_flat = top_idx.reshape(A)
    tok_flat = jnp.arange(A) // TOP_K
    g_flat = gates.reshape(A)
    order = jnp.argsort(e_flat)
    e_sorted = e_flat[order]
    tok_sorted = tok_flat[order]
    counts = jnp.bincount(e_flat, length=N_EXPERTS)
    padded = (counts + MOE_BLOCK - 1) // MOE_BLOCK * MOE_BLOCK
    start = jnp.cumsum(counts) - counts
    pend = jnp.cumsum(padded)
    pstart = pend - padded
    dest = pstart[e_sorted] + (jnp.arange(A) - start[e_sorted])
    P = ((A + N_EXPERTS * MOE_BLOCK + MOE_BLOCK - 1) // MOE_BLOCK) * MOE_BLOCK
    nblk = P // MOE_BLOCK
    buf = jnp.zeros((P, D), h.dtype).at[dest].set(xt[tok_sorted])
    blk_e = jnp.minimum(jnp.searchsorted(pend, jnp.arange(nblk) * MOE_BLOCK, side='right'), N_EXPERTS - 1)

    def expert_block(args):
        xb, e = args
        return (jax.nn.silu(xb @ w_gate[e]) * (xb @ w_up[e])) @ w_down[e]

    ybuf = lax.map(expert_block, (buf.reshape(nblk, MOE_BLOCK, D), blk_e)).reshape(P, D)
    y = ybuf[dest] * g_flat[order][:, None].astype(h.dtype)
    out = jnp.zeros((N, D), h.dtype).at[tok_sorted].add(y)
    return out.reshape(Bn, S, D)


def setup_inputs(seed: int = 0) -> dict:
    key = jax.random.key(seed)
    keys = jax.random.split(key, 32)
    ks = iter([keys[i] for i in range(32)])
    f32 = jnp.float32

    def dense(shape, fan_in):
        return jax.random.normal(next(ks), shape, f32) * (fan_in ** -0.5)

    def gain(shape):
        return 1.0 + 0.02 * jax.random.normal(next(ks), shape, f32)

    def small(shape, s):
        return s * jax.random.normal(next(ks), shape, f32)

    x = jax.random.normal(next(ks), (BATCH, SEQ, D_MODEL), f32)
    offsets = jax.random.randint(next(ks), (BATCH, 1), 0, 4096, dtype=jnp.int32)
    positions = offsets + jnp.arange(SEQ, dtype=jnp.int32)[None, :]
    return {
        'x': x,
        'positions': positions,
        'norm_mix': gain((DEPTH, D_MODEL)),
        'w_in': dense((DEPTH, D_MODEL, IN_COLS), D_MODEL),
        'b_igate': small((DEPTH, A_HEADS), 0.1),
        'b_fgate': jnp.linspace(3.0, 6.0, A_HEADS, dtype=f32)[None, :] + small((DEPTH, A_HEADS), 0.1),
        'conv_w': dense((DEPTH, A_CONV, 2 * A_QK_W), A_CONV),
        'conv_b': small((DEPTH, 2 * A_QK_W), 0.02),
        'mlstm_norm': gain((DEPTH, A_HEADS, A_V)),
        'w_branch_a': dense((DEPTH, A_V_W, D_MODEL), A_V_W),
        'q_norm': gain((DEPTH, Q_LORA)),
        'kv_norm': gain((DEPTH, KV_LORA)),
        'w_uq': dense((DEPTH, Q_LORA, B_Q_W), Q_LORA),
        'w_ukv': dense((DEPTH, KV_LORA, B_KV_W), KV_LORA),
        'w_branch_b': dense((DEPTH, B_OUT_W, D_MODEL), B_OUT_W),
        'w_out': dense((DEPTH, D_MODEL, D_MODEL), D_MODEL),
        'norm_ffn': gain((DEPTH, D_MODEL)),
        'w_gate_d': dense((N_DENSE, D_MODEL, D_FF), D_MODEL),
        'w_up_d': dense((N_DENSE, D_MODEL, D_FF), D_MODEL),
        'w_down_d': dense((N_DENSE, D_FF, D_MODEL), D_FF),
        'w_router': dense((N_MOE, D_MODEL, N_EXPERTS), D_MODEL),
        'w_gate_e': dense((N_MOE, N_EXPERTS, D_MODEL, D_FF_EXPERT), D_MODEL),
        'w_up_e': dense((N_MOE, N_EXPERTS, D_MODEL, D_FF_EXPERT), D_MODEL),
        'w_down_e': dense((N_MOE, N_EXPERTS, D_FF_EXPERT, D_MODEL), D_FF_EXPERT),
        'norm_final': gain((D_MODEL,)),
    }


def reference(x, positions, norm_mix, w_in, b_igate, b_fgate, conv_w, conv_b, mlstm_norm, w_branch_a,
              q_norm, kv_norm, w_uq, w_ukv, w_branch_b, w_out, norm_ffn, w_gate_d, w_up_d, w_down_d,
              w_router, w_gate_e, w_up_e, w_down_e, norm_final):
    cos, sin = rope_tables(positions)
    for l in range(DEPTH):
        h = rmsnorm(x, norm_mix[l])
        x = x + hybrid_mixer(h, cos, sin, w_in[l], b_igate[l], b_fgate[l], conv_w[l], conv_b[l],
                             mlstm_norm[l], w_branch_a[l], q_norm[l], kv_norm[l], w_uq[l], w_ukv[l],
                             w_branch_b[l], w_out[l])
        h = rmsnorm(x, norm_ffn[l])
        if l % 2 == 0:
            x = x + swiglu(h, w_gate_d[l // 2], w_up_d[l // 2], w_down_d[l // 2])
        else:
            x = x + moe_swiglu(h, w_router[l // 2], w_gate_e[l // 2], w_up_e[l // 2], w_down_e[l // 2])
    return rmsnorm(x, norm_final)
```

```python
import functools

import jax
import jax.numpy as jnp
from jax import lax
from jax.experimental import pallas as pl
from jax.experimental.pallas import tpu as pltpu

F32 = jnp.float32
BF16 = jnp.bfloat16

EPS = 1e-6
CHUNK = 64
A_HEADS = 8
A_QK = 128
A_V = 256
B_HEADS = 16
B_NOPE = 128
B_ROPE = 64
B_V = 128
Q_LORA = 1024
KV_LORA = 512
ROPE_BASE = 10000.0
N_EXPERTS = 8
TOP_K = 2

LANES = 128
VMEM_LIMIT = 56 * 1024 * 1024


def _cparams(sem):
    return pltpu.CompilerParams(dimension_semantics=sem, vmem_limit_bytes=VMEM_LIMIT)


def _rmsnorm_kernel(x_ref, g_ref, o_ref):
    x = x_ref[...].astype(F32)
    r = lax.rsqrt(jnp.mean(x * x, axis=-1, keepdims=True) + EPS)
    o_ref[...] = (x * r * g_ref[...]).astype(o_ref.dtype)


def _rmsnorm(x, g, out_dtype, tm=512):
    m, d = x.shape
    return pl.pallas_call(
        _rmsnorm_kernel,
        grid=(m // tm,),
        in_specs=[pl.BlockSpec((tm, d), lambda i: (i, 0)),
                  pl.BlockSpec((1, d), lambda i: (0, 0))],
        out_specs=pl.BlockSpec((tm, d), lambda i: (i, 0)),
        out_shape=jax.ShapeDtypeStruct((m, d), out_dtype),
        compiler_params=_cparams(("parallel",)),
        name="rmsnorm",
    )(x, g.reshape(1, d).astype(F32))


def _mm_kernel(a_ref, w_ref, o_ref):
    o_ref[...] = jnp.dot(a_ref[...], w_ref[...], preferred_element_type=F32).astype(o_ref.dtype)


def _mm_res_kernel(a_ref, w_ref, r_ref, o_ref):
    acc = jnp.dot(a_ref[...], w_ref[...], preferred_element_type=F32)
    o_ref[...] = (r_ref[...] + acc).astype(o_ref.dtype)


def _matmul(a, w, n_out, out_dtype, tm, tn, residual=None, name="matmul"):
    m, k = a.shape
    tm, tn = min(tm, m), min(tn, n_out)
    grid = (m // tm, n_out // tn)
    in_specs = [pl.BlockSpec((tm, k), lambda i, j: (i, 0)),
                pl.BlockSpec((k, tn), lambda i, j: (0, j))]
    args = [a, w]
    kern = _mm_kernel
    if residual is not None:
        in_specs.append(pl.BlockSpec((tm, tn), lambda i, j: (i, j)))
        args.append(residual)
        kern = _mm_res_kernel
    return pl.pallas_call(
        kern,
        grid=grid,
        in_specs=in_specs,
        out_specs=pl.BlockSpec((tm, tn), lambda i, j: (i, j)),
        out_shape=jax.ShapeDtypeStruct((m, n_out), out_dtype),
        compiler_params=_cparams(("parallel", "parallel")),
        name=name,
    )(*args)


def _mm_merge_kernel(a_ref, w_ref, ya_ref, ga_ref, gb_ref, o_ref):
    yb = jnp.dot(a_ref[...], w_ref[...], preferred_element_type=F32)
    merged = jax.nn.sigmoid(ga_ref[...]) * ya_ref[...] + jax.nn.sigmoid(gb_ref[...]) * yb
    o_ref[...] = merged.astype(o_ref.dtype)


def _matmul_merge(a, w, y_a, z_tail, ga_off, gb_off, tm, tn):
    m, k = a.shape
    n = w.shape[1]
    ga_blk, gb_blk = ga_off // tn, gb_off // tn
    return pl.pallas_call(
        _mm_merge_kernel,
        grid=(m // tm, n // tn),
        in_specs=[pl.BlockSpec((tm, k), lambda i, j: (i, 0)),
                  pl.BlockSpec((k, tn), lambda i, j: (0, j)),
                  pl.BlockSpec((tm, tn), lambda i, j: (i, j)),
                  pl.BlockSpec((tm, tn), lambda i, j: (i, j + ga_blk)),
                  pl.BlockSpec((tm, tn), lambda i, j: (i, j + gb_blk))],
        out_specs=pl.BlockSpec((tm, tn), lambda i, j: (i, j)),
        out_shape=jax.ShapeDtypeStruct((m, n), BF16),
        compiler_params=_cparams(("parallel", "parallel")),
        name="branch_b_merge",
    )(a, w, y_a, z_tail, z_tail)


def _rope_table_kernel(pos_ref, inv_ref, o_ref):
    ang = pos_ref[...].astype(F32) * inv_ref[...]
    lane = lax.broadcasted_iota(jnp.int32, ang.shape, 1)
    o_ref[...] = jnp.where(lane < B_ROPE, jnp.cos(ang), jnp.sin(ang))


def _rope_table(positions, tm=1024):
    n = positions.size
    inv = ROPE_BASE ** (-jnp.arange(0, B_ROPE, 2, dtype=F32) / B_ROPE)
    inv4 = jnp.tile(inv, 4).reshape(1, LANES)
    return pl.pallas_call(
        _rope_table_kernel,
        grid=(n // tm,),
        in_specs=[pl.BlockSpec((tm, 1), lambda i: (i, 0)),
                  pl.BlockSpec((1, LANES), lambda i: (0, 0))],
        out_specs=pl.BlockSpec((tm, LANES), lambda i: (i, 0)),
        out_shape=jax.ShapeDtypeStruct((n, LANES), F32),
        compiler_params=_cparams(("parallel",)),
        name="rope_table",
    )(positions.reshape(n, 1), inv4)


def _log_sigmoid(x):
    return jnp.minimum(x, 0.0) - jnp.log1p(jnp.exp(-jnp.abs(x)))


def _mlstm_kernel(zqk_ref, zv_ref, zo_ref, g_ref, cw_ref, cb_ref, gbias_ref, nw_ref, o_ref,
                  ct_ref, n_ref, m_ref, prev_ref):
    H, dk, dv, L = A_HEADS, A_QK, A_V, CHUNK
    c = pl.program_id(1)

    @pl.when(c == 0)
    def _():
        ct_ref[...] = jnp.zeros_like(ct_ref)
        n_ref[...] = jnp.zeros_like(n_ref)
        m_ref[...] = jnp.zeros_like(m_ref)
        prev_ref[...] = jnp.zeros_like(prev_ref)

    u = zqk_ref[...]
    prev = prev_ref[...]
    kc = cw_ref.shape[0]
    row = lax.broadcasted_iota(jnp.int32, u.shape, 0)
    y = u * cw_ref[kc - 1:kc, :] + cb_ref[...]
    for d in range(1, kc):
        shifted = jnp.where(row < d, pltpu.roll(prev, d, axis=0), pltpu.roll(u, d, axis=0))
        y = y + shifted * cw_ref[kc - 1 - d:kc - d, :]
    prev_ref[...] = u
    qk = y * jax.nn.sigmoid(y)

    gb = g_ref[...] + gbias_ref[...]
    gt = gb.T
    lsg = _log_sigmoid(gb)
    lsgt = _log_sigmoid(gt)

    ri = lax.broadcasted_iota(jnp.int32, (L, L), 0)
    ci = lax.broadcasted_iota(jnp.int32, (L, L), 1)
    tril = ci <= ri

    for h in range(H):
        ig_c = gb[:, h:h + 1]
        lf_c = lsg[:, H + h:H + h + 1]
        ig_r = gt[h:h + 1, :]
        lf_r = lsgt[H + h:H + h + 1, :]
        b_c = jnp.sum(jnp.where(tril, lf_r, 0.0), axis=1, keepdims=True)
        b_r = jnp.sum(jnp.where(ri <= ci, lf_c, 0.0), axis=0, keepdims=True)
        b_last = jnp.sum(lf_r, axis=1, keepdims=True)
        a_r = b_last - b_r + ig_r
        a_c = b_last - b_c + ig_c
        m_loc = jnp.max(a_r, axis=1, keepdims=True)
        wa_c = jnp.exp(a_c - m_loc)

        q = qk[:, h * dk:(h + 1) * dk]
        k = qk[:, (H + h) * dk:(H + h + 1) * dk] * (dk ** -0.5)
        v = zv_ref[:, h * dv:(h + 1) * dv]
        q_b = q.astype(BF16)
        k_b = k.astype(BF16)

        m_st = m_ref[h]
        ct = ct_ref[h]
        nvec = n_ref[h]

        dm = jnp.where(tril, b_c - b_r + ig_r, -jnp.inf)
        m_inter = b_c + m_st
        m_j = jnp.maximum(m_inter, jnp.max(dm, axis=1, keepdims=True))
        s = lax.dot_general(q_b, k_b, (((1,), (1,)), ((), ())), preferred_element_type=F32)
        qkw = s * jnp.exp(dm - m_j)
        s_inter = jnp.exp(m_inter - m_j)
        num = (jnp.dot(qkw.astype(BF16), v.astype(BF16), preferred_element_type=F32)
               + s_inter * jnp.dot(q_b, ct.astype(BF16), preferred_element_type=F32))
        den = (jnp.sum(qkw, axis=1, keepdims=True)
               + s_inter * jnp.sum(q * nvec, axis=1, keepdims=True))
        hh = num / jnp.maximum(jnp.abs(den), jnp.exp(-m_j))

        r = lax.rsqrt(jnp.mean(hh * hh, axis=-1, keepdims=True) + EPS)
        ha = hh * r * nw_ref[h]
        og = jax.nn.sigmoid(zo_ref[:, h * dv:(h + 1) * dv])
        o_ref[:, h * dv:(h + 1) * dv] = (og * ha).astype(o_ref.dtype)

        vw = (v * wa_c).astype(BF16)
        ct_loc = jnp.dot(k.T.astype(BF16), vw, preferred_element_type=F32)
        n_loc = jnp.sum(k * wa_c, axis=0, keepdims=True)
        m_new = jnp.maximum(b_last + m_st, m_loc)
        s_old = jnp.exp(b_last + m_st - m_new)
        s_loc = jnp.exp(m_loc - m_new)
        ct_ref[h] = s_old * ct + s_loc * ct_loc
        n_ref[h] = s_old * nvec + s_loc * n_loc
        m_ref[h] = m_new


def _mlstm(z_main, z_tail, gate_blk, conv_w, conv_b, gbias, norm_w, batch, seq):
    H, dk, dv, L = A_HEADS, A_QK, A_V, CHUNK
    n = batch * seq
    nc = seq // L
    qk_w, v_w = 2 * H * dk, H * dv
    assert qk_w == v_w
    row = lambda b, c: b * nc + c
    return pl.pallas_call(
        _mlstm_kernel,
        grid=(batch, nc),
        in_specs=[pl.BlockSpec((L, qk_w), lambda b, c: (row(b, c), 0)),
                  pl.BlockSpec((L, v_w), lambda b, c: (row(b, c), 1)),
                  pl.BlockSpec((L, v_w), lambda b, c: (row(b, c), 2)),
                  pl.BlockSpec((L, LANES), lambda b, c: (row(b, c), gate_blk)),
                  pl.BlockSpec(conv_w.shape, lambda b, c: (0, 0)),
                  pl.BlockSpec((1, qk_w), lambda b, c: (0, 0)),
                  pl.BlockSpec((1, LANES), lambda b, c: (0, 0)),
                  pl.BlockSpec((H, 1, dv), lambda b, c: (0, 0, 0))],
        out_specs=pl.BlockSpec((L, v_w), lambda b, c: (row(b, c), 0)),
        out_shape=jax.ShapeDtypeStruct((n, v_w), BF16),
        scratch_shapes=[pltpu.VMEM((H, dk, dv), F32),
                        pltpu.VMEM((H, 1, dk), F32),
                        pltpu.VMEM((H, 1, 1), F32),
                        pltpu.VMEM((L, qk_w), F32)],
        compiler_params=_cparams(("parallel", "arbitrary")),
        name="mlstm",
    )(z_main, z_main, z_main, z_tail, conv_w, conv_b.reshape(1, qk_w), gbias, norm_w.reshape(H, 1, dv))


def _mla_prep_kernel(cq_ref, ckv_ref, kr_ref, cs_ref, qn_ref, kvn_ref, cqn_ref, ckvn_ref, kro_ref):
    cq = cq_ref[...]
    r = lax.rsqrt(jnp.mean(cq * cq, axis=-1, keepdims=True) + EPS)
    cqn_ref[...] = (cq * r * qn_ref[...]).astype(cqn_ref.dtype)
    ckv = ckv_ref[...]
    r = lax.rsqrt(jnp.mean(ckv * ckv, axis=-1, keepdims=True) + EPS)
    ckvn_ref[...] = (ckv * r * kvn_ref[...]).astype(ckvn_ref.dtype)
    t = kr_ref[...] * cs_ref[...]
    kro_ref[...] = t + pltpu.roll(t, B_ROPE, axis=1)


def _mla_prep(z_tail, cs, q_norm, kv_norm, cq_off, ckv_off, kr_off, tm=512):
    n = z_tail.shape[0]
    return pl.pallas_call(
        _mla_prep_kernel,
        grid=(n // tm,),
        in_specs=[pl.BlockSpec((tm, Q_LORA), lambda i: (i, cq_off // Q_LORA)),
                  pl.BlockSpec((tm, KV_LORA), lambda i: (i, ckv_off // KV_LORA)),
                  pl.BlockSpec((tm, LANES), lambda i: (i, kr_off // LANES)),
                  pl.BlockSpec((tm, LANES), lambda i: (i, 0)),
                  pl.BlockSpec((1, Q_LORA), lambda i: (0, 0)),
                  pl.BlockSpec((1, KV_LORA), lambda i: (0, 0))],
        out_specs=[pl.BlockSpec((tm, Q_LORA), lambda i: (i, 0)),
                   pl.BlockSpec((tm, KV_LORA), lambda i: (i, 0)),
                   pl.BlockSpec((tm, LANES), lambda i: (i, 0))],
        out_shape=[jax.ShapeDtypeStruct((n, Q_LORA), BF16),
                   jax.ShapeDtypeStruct((n, KV_LORA), BF16),
                   jax.ShapeDtypeStruct((n, LANES), F32)],
        compiler_params=_cparams(("parallel",)),
        name="mla_prep",
    )(z_tail, z_tail, z_tail, cs, q_norm.reshape(1, Q_LORA), kv_norm.reshape(1, KV_LORA))


def _uq_kernel(a_ref, w_ref, cs_ref, o_ref):
    scale = (B_NOPE + B_ROPE) ** -0.5
    acc = jnp.dot(a_ref[...], w_ref[0], preferred_element_type=F32)
    t = acc[:, B_NOPE:] * cs_ref[...]
    rot = t + pltpu.roll(t, B_ROPE, axis=1)
    o_ref[0, :, :B_NOPE] = (acc[:, :B_NOPE] * scale).astype(o_ref.dtype)
    o_ref[0, :, B_NOPE:] = (rot[:, :B_ROPE] * scale).astype(o_ref.dtype)


def _uq_proj(cqn, w_q, cs, tm=512):
    n = cqn.shape[0]
    dq = B_NOPE + B_ROPE
    return pl.pallas_call(
        _uq_kernel,
        grid=(n // tm, B_HEADS),
        in_specs=[pl.BlockSpec((tm, Q_LORA), lambda i, h: (i, 0)),
                  pl.BlockSpec((1, Q_LORA, 2 * LANES), lambda i, h: (h, 0, 0)),
                  pl.BlockSpec((tm, LANES), lambda i, h: (i, 0))],
        out_specs=pl.BlockSpec((1, tm, dq), lambda i, h: (h, i, 0)),
        out_shape=jax.ShapeDtypeStruct((B_HEADS, n, dq), BF16),
        compiler_params=_cparams(("parallel", "parallel")),
        name="uq_proj",
    )(cqn, w_q, cs)


def _ukv_kernel(a_ref, w_ref, kr_ref, k_ref, v_ref):
    acc = jnp.dot(a_ref[...], w_ref[0], preferred_element_type=F32)
    k_ref[0, :, :B_NOPE] = acc[:, :B_NOPE].astype(k_ref.dtype)
    k_ref[0, :, B_NOPE:] = kr_ref[:, :B_ROPE].astype(k_ref.dtype)
    v_ref[0] = acc[:, B_NOPE:].astype(v_ref.dtype)


def _ukv_proj(ckvn, w_kv, kr, tm=512):
    n = ckvn.shape[0]
    dq = B_NOPE + B_ROPE
    return pl.pallas_call(
        _ukv_kernel,
        grid=(n // tm, B_HEADS),
        in_specs=[pl.BlockSpec((tm, KV_LORA), lambda i, h: (i, 0)),
                  pl.BlockSpec((1, KV_LORA, B_NOPE + B_V), lambda i, h: (h, 0, 0)),
                  pl.BlockSpec((tm, LANES), lambda i, h: (i, 0))],
        out_specs=[pl.BlockSpec((1, tm, dq), lambda i, h: (h, i, 0)),
                   pl.BlockSpec((1, tm, B_V), lambda i, h: (h, i, 0))],
        out_shape=[jax.ShapeDtypeStruct((B_HEADS, n, dq), BF16),
                   jax.ShapeDtypeStruct((B_HEADS, n, B_V), BF16)],
        compiler_params=_cparams(("parallel", "parallel")),
        name="ukv_proj",
    )(ckvn, w_kv, kr)


def _attn_kernel(q_ref, k_ref, v_ref, o_ref, *, tq):
    i = pl.program_id(2)
    q = q_ref[0]
    dv = v_ref.shape[-1]

    def step(off, carry, mask):
        m, l, acc = carry
        kb = k_ref[0, pl.ds(off, tq), :]
        vb = v_ref[0, pl.ds(off, tq), :]
        s = lax.dot_general(q, kb, (((1,), (1,)), ((), ())), preferred_element_type=F32)
        if mask is not None:
            s = jnp.where(mask, s, -jnp.inf)
        m_new = jnp.maximum(m, jnp.max(s, axis=-1, keepdims=True))
        p = jnp.exp(s - m_new)
        alpha = jnp.exp(m - m_new)
        l = alpha * l + jnp.sum(p, axis=-1, keepdims=True)
        acc = alpha * acc + jnp.dot(p.astype(BF16), vb, preferred_element_type=F32)
        return m_new, l, acc

    init = (jnp.full((tq, 1), -jnp.inf, F32), jnp.zeros((tq, 1), F32), jnp.zeros((tq, dv), F32))
    carry = lax.fori_loop(0, i, lambda j, c: step(pl.multiple_of(j * tq, tq), c, None), init)
    qc = lax.broadcasted_iota(jnp.int32, (tq, tq), 0) // CHUNK
    kc = lax.broadcasted_iota(jnp.int32, (tq, tq), 1) // CHUNK
    m, l, acc = step(pl.multiple_of(i * tq, tq), carry, kc <= qc)
    o_ref[...] = (acc / l).astype(o_ref.dtype)


def _attention(q, k, v, batch, seq, tq=512):
    nq = seq // tq
    n = batch * seq
    dq = q.shape[-1]
    return pl.pallas_call(
        functools.partial(_attn_kernel, tq=tq),
        grid=(batch, B_HEADS, nq),
        in_specs=[pl.BlockSpec((1, tq, dq), lambda b, h, i: (h, b * nq + i, 0)),
                  pl.BlockSpec((1, seq, dq), lambda b, h, i: (h, b, 0)),
                  pl.BlockSpec((1, seq, B_V), lambda b, h, i: (h, b, 0))],
        out_specs=pl.BlockSpec((tq, B_V), lambda b, h, i: (b * nq + i, h)),
        out_shape=jax.ShapeDtypeStruct((n, B_HEADS * B_V), BF16),
        compiler_params=_cparams(("parallel", "parallel", "arbitrary")),
        name="attention",
    )(q, k, v)


def _ffn_kernel(be_ref, nu_ref, x_ref, wg_ref, wu_ref, wd_ref, *rest, has_res):
    o_ref = rest[-1]
    i = pl.program_id(0)
    f = pl.program_id(1)
    used = i < nu_ref[0]

    @pl.when(used)
    def _():
        x = x_ref[...].astype(BF16)
        g = jnp.dot(x, wg_ref[0], preferred_element_type=F32)
        u = jnp.dot(x, wu_ref[0], preferred_element_type=F32)
        h1 = (g * jax.nn.sigmoid(g) * u).astype(BF16)
        part = jnp.dot(h1, wd_ref[0], preferred_element_type=F32)

        @pl.when(f == 0)
        def _():
            o_ref[...] = (rest[0][...] + part) if has_res else part

        @pl.when(f > 0)
        def _():
            o_ref[...] += part

    @pl.when(jnp.logical_and(jnp.logical_not(used), f == 0))
    def _():
        o_ref[...] = jnp.zeros_like(o_ref)


def _ffn(x, w_gate, w_up, w_down, blk_e, n_used, residual, tm, tf):
    m, d = x.shape
    fdim = w_gate.shape[-1]
    nblk = m // tm

    def row_map(i, f, be, nu):
        return (jnp.minimum(i, nu[0] - 1), 0)

    in_specs = [pl.BlockSpec((tm, d), row_map, pipeline_mode=pl.Buffered(1)),
                pl.BlockSpec((1, d, tf), lambda i, f, be, nu: (be[i], 0, f)),
                pl.BlockSpec((1, d, tf), lambda i, f, be, nu: (be[i], 0, f)),
                pl.BlockSpec((1, tf, d), lambda i, f, be, nu: (be[i], f, 0))]
    args = [x, w_gate, w_up, w_down]
    if residual is not None:
        in_specs.append(pl.BlockSpec((tm, d), row_map, pipeline_mode=pl.Buffered(1)))
        args.append(residual)
    return pl.pallas_call(
        functools.partial(_ffn_kernel, has_res=residual is not None),
        grid_spec=pltpu.PrefetchScalarGridSpec(
            num_scalar_prefetch=2,
            grid=(nblk, fdim // tf),
            in_specs=in_specs,
            out_specs=pl.BlockSpec((tm, d), lambda i, f, be, nu: (i, 0)),
        ),
        out_shape=jax.ShapeDtypeStruct((m, d), F32),
        compiler_params=_cparams(("parallel", "arbitrary")),
        name="swiglu",
    )(blk_e, n_used, *args)


def _router_kernel(x_ref, w_ref, idx_ref, gate_ref):
    x = x_ref[...]
    w = w_ref[...]
    x_hi = x.astype(BF16)
    x_lo = (x - x_hi.astype(F32)).astype(BF16)
    w_hi = w.astype(BF16)
    w_lo = (w - w_hi.astype(F32)).astype(BF16)
    logits = (jnp.dot(x_hi, w_hi, preferred_element_type=F32)
              + jnp.dot(x_hi, w_lo, preferred_element_type=F32)
              + jnp.dot(x_lo, w_hi, preferred_element_type=F32))
    lane = lax.broadcasted_iota(jnp.int32, logits.shape, 1)
    logits = jnp.where(lane < N_EXPERTS, logits, -jnp.inf)
    m1 = jnp.max(logits, axis=-1, keepdims=True)
    i1 = jnp.min(jnp.where(logits == m1, lane, LANES), axis=-1, keepdims=True)
    rest = jnp.where(lane == i1, -jnp.inf, logits)
    m2 = jnp.max(rest, axis=-1, keepdims=True)
    i2 = jnp.min(jnp.where(rest == m2, lane, LANES), axis=-1, keepdims=True)
    e2 = jnp.exp(m2 - m1)
    g1 = 1.0 / (1.0 + e2)
    g2 = e2 / (1.0 + e2)
    idx_ref[...] = jnp.where(lane == 0, i1, jnp.where(lane == 1, i2, 0))
    gate_ref[...] = jnp.where(lane == 0, g1, jnp.where(lane == 1, g2, 0.0))


def _router(h, w_router, tm=512):
    n, d = h.shape
    w_pad = jnp.zeros((d, LANES), F32).at[:, :N_EXPERTS].set(w_router)
    idx, gate = pl.pallas_call(
        _router_kernel,
        grid=(n // tm,),
        in_specs=[pl.BlockSpec((tm, d), lambda i: (i, 0)),
                  pl.BlockSpec((d, LANES), lambda i: (0, 0))],
        out_specs=[pl.BlockSpec((tm, LANES), lambda i: (i, 0)),
                   pl.BlockSpec((tm, LANES), lambda i: (i, 0))],
        out_shape=[jax.ShapeDtypeStruct((n, LANES), jnp.int32),
                   jax.ShapeDtypeStruct((n, LANES), F32)],
        compiler_params=_cparams(("parallel",)),
        name="router",
    )(h, w_pad)
    return idx[:, :TOP_K], gate


def _gather_kernel(src_ref, x_hbm, o_ref, sem, *, tr):
    base = pl.program_id(0) * tr

    def copy(r):
        return pltpu.make_async_copy(x_hbm.at[pl.ds(src_ref[base + r], 1)], o_ref.at[pl.ds(r, 1)], sem)

    def start(r, carry):
        copy(r).start()
        return carry

    def wait(r, carry):
        copy(r).wait()
        return carry

    lax.fori_loop(0, tr, start, 0)
    lax.fori_loop(0, tr, wait, 0)


def _gather_rows(x, src, tr=256):
    p = src.shape[0]
    d = x.shape[1]
    return pl.pallas_call(
        functools.partial(_gather_kernel, tr=tr),
        grid_spec=pltpu.PrefetchScalarGridSpec(
            num_scalar_prefetch=1,
            grid=(p // tr,),
            in_specs=[pl.BlockSpec(memory_space=pl.ANY)],
            out_specs=pl.BlockSpec((tr, d), lambda i, s: (i, 0)),
            scratch_shapes=[pltpu.SemaphoreType.DMA(())],
        ),
        out_shape=jax.ShapeDtypeStruct((p, d), x.dtype),
        compiler_params=_cparams(("arbitrary",)),
        name="moe_gather",
    )(src, x)


def _combine_kernel(p0_ref, p1_ref, y_hbm, x_ref, g_ref, nw_ref, o_ref, b0_ref, b1_ref, sem, *, tc, normalize):
    base = pl.program_id(0) * tc

    def copies(r):
        return (pltpu.make_async_copy(y_hbm.at[pl.ds(p0_ref[base + r], 1)], b0_ref.at[pl.ds(r, 1)], sem.at[0]),
                pltpu.make_async_copy(y_hbm.at[pl.ds(p1_ref[base + r], 1)], b1_ref.at[pl.ds(r, 1)], sem.at[1]))

    def start(r, carry):
        c0, c1 = copies(r)
        c0.start()
        c1.start()
        return carry

    def wait(r, carry):
        c0, c1 = copies(r)
        c0.wait()
        c1.wait()
        return carry

    lax.fori_loop(0, tc, start, 0)
    lax.fori_loop(0, tc, wait, 0)
    g = g_ref[...]
    x = x_ref[...] + g[:, 0:1] * b0_ref[...] + g[:, 1:2] * b1_ref[...]
    if normalize:
        r = lax.rsqrt(jnp.mean(x * x, axis=-1, keepdims=True) + EPS)
        x = x * r * nw_ref[...]
    o_ref[...] = x


def _combine(y_sorted, x, gates, pos0, pos1, norm_w, normalize, tc=256):
    n, d = x.shape
    return pl.pallas_call(
        functools.partial(_combine_kernel, tc=tc, normalize=normalize),
        grid_spec=pltpu.PrefetchScalarGridSpec(
            num_scalar_prefetch=2,
            grid=(n // tc,),
            in_specs=[pl.BlockSpec(memory_space=pl.ANY),
                      pl.BlockSpec((tc, d), lambda i, a, b: (i, 0)),
                      pl.BlockSpec((tc, LANES), lambda i, a, b: (i, 0)),
                      pl.BlockSpec((1, d), lambda i, a, b: (0, 0))],
            out_specs=pl.BlockSpec((tc, d), lambda i, a, b: (i, 0)),
            scratch_shapes=[pltpu.VMEM((tc, d), F32), pltpu.VMEM((tc, d), F32),
                            pltpu.SemaphoreType.DMA((2,))],
        ),
        out_shape=jax.ShapeDtypeStruct((n, d), F32),
        compiler_params=_cparams(("arbitrary",)),
        name="moe_combine",
    )(pos0, pos1, y_sorted, x, gates, norm_w.reshape(1, d))


def _moe_plan(idx, tm):
    n = idx.shape[0]
    a = n * TOP_K
    e_flat = idx.reshape(a)
    onehot = (e_flat[:, None] == jnp.arange(N_EXPERTS, dtype=jnp.int32)[None, :]).astype(jnp.int32)
    csum = jnp.cumsum(onehot, axis=0)
    rank = jnp.sum(onehot * (csum - 1), axis=1)
    counts = csum[-1]
    padded = (counts + tm - 1) // tm * tm
    pend = jnp.cumsum(padded)
    pstart = pend - padded
    dest = (pstart[e_flat] + rank).astype(jnp.int32)
    p = a + N_EXPERTS * tm
    nblk = p // tm
    src = jnp.zeros((p,), jnp.int32).at[dest].set(jnp.arange(a, dtype=jnp.int32) // TOP_K)
    blk_e = jnp.minimum(jnp.searchsorted(pend, jnp.arange(nblk, dtype=jnp.int32) * tm, side="right"),
                        N_EXPERTS - 1).astype(jnp.int32)
    n_used = (pend[-1:] // tm).astype(jnp.int32)
    pos = dest.reshape(n, TOP_K)
    return src, blk_e, n_used, pos[:, 0], pos[:, 1]


def _in_proj_tail(w_in_l):
    o = 2 * A_HEADS * A_QK + 2 * A_HEADS * A_V
    a_i = w_in_l[:, o:o + A_HEADS]
    a_f = w_in_l[:, o + A_HEADS:o + 2 * A_HEADS]
    o += 2 * A_HEADS
    c_q = w_in_l[:, o:o + Q_LORA]
    o += Q_LORA
    c_kv = w_in_l[:, o:o + KV_LORA]
    o += KV_LORA
    k_r = w_in_l[:, o:o + B_ROPE]
    o += B_ROPE
    d = w_in_l.shape[0]
    g_a = w_in_l[:, o:o + d]
    g_b = w_in_l[:, o + d:o + 2 * d]
    half = B_ROPE // 2
    k_r_sw = jnp.concatenate([-k_r[:, half:], k_r[:, :half]], axis=1)
    used = 2 * d + Q_LORA + KV_LORA + 2 * B_ROPE + 2 * A_HEADS
    pad = jnp.zeros((d, -used % 512), w_in_l.dtype)
    tail = jnp.concatenate([g_a, g_b, c_q, c_kv, k_r, k_r_sw, a_i, a_f, pad], axis=1)
    offs = dict(g_a=0, g_b=d, c_q=2 * d, c_kv=2 * d + Q_LORA, k_r=2 * d + Q_LORA + KV_LORA,
                gates=2 * d + Q_LORA + KV_LORA + LANES)
    return tail.astype(BF16), offs


def _uq_weight(w_uq_l):
    w = w_uq_l.reshape(Q_LORA, B_HEADS, B_NOPE + B_ROPE)
    nope, rope = w[..., :B_NOPE], w[..., B_NOPE:]
    half = B_ROPE // 2
    rope_sw = jnp.concatenate([-rope[..., half:], rope[..., :half]], axis=-1)
    return jnp.transpose(jnp.concatenate([nope, rope, rope_sw], axis=-1), (1, 0, 2)).astype(BF16)


def _ukv_weight(w_ukv_l):
    w = w_ukv_l.reshape(KV_LORA, B_HEADS, B_NOPE + B_V)
    return jnp.transpose(w, (1, 0, 2)).astype(BF16)


def _mixer(x, cs, batch, seq, norm_w, w_in_l, b_i, b_f, conv_w, conv_b, mlstm_norm, w_a,
           q_norm, kv_norm, w_uq_l, w_ukv_l, w_b, w_out_l):
    d = x.shape[1]
    h = _rmsnorm(x, norm_w, BF16)
    main_w = 2 * A_HEADS * A_QK + 2 * A_HEADS * A_V
    z_main = _matmul(h, w_in_l.astype(BF16), main_w, F32, tm=1024, tn=512, name="in_proj_main")
    w_tail, offs = _in_proj_tail(w_in_l)
    z_tail = _matmul(h, w_tail, w_tail.shape[1], F32, tm=1024, tn=512, name="in_proj_tail")

    gbias = jnp.zeros((1, LANES), F32).at[0, :A_HEADS].set(b_i).at[0, A_HEADS:2 * A_HEADS].set(b_f)
    h_a = _mlstm(z_main, z_tail, offs["gates"] // LANES, conv_w, conv_b, gbias, mlstm_norm, batch, seq)
    y_a = _matmul(h_a, w_a.astype(BF16), d, F32, tm=1024, tn=512, name="branch_a")

    cqn, ckvn, kr = _mla_prep(z_tail, cs, q_norm, kv_norm, offs["c_q"], offs["c_kv"], offs["k_r"])
    q = _uq_proj(cqn, _uq_weight(w_uq_l), cs)
    k, v = _ukv_proj(ckvn, _ukv_weight(w_ukv_l), kr)
    att = _attention(q, k, v, batch, seq)
    merged = _matmul_merge(att, w_b.astype(BF16), y_a, z_tail, offs["g_a"], offs["g_b"], tm=512, tn=512)
    return _matmul(merged, w_out_l.astype(BF16), d, F32, tm=1024, tn=512, residual=x, name="out_proj")


def kernel(x, positions, norm_mix, w_in, b_igate, b_fgate, conv_w, conv_b, mlstm_norm, w_branch_a, q_norm, kv_norm, w_uq, w_ukv, w_branch_b, w_out, norm_ffn, w_gate_d, w_up_d, w_down_d, w_router, w_gate_e, w_up_e, w_down_e, norm_final):
    batch, seq, d = x.shape
    n = batch * seq
    depth = norm_mix.shape[0]
    x = x.reshape(n, d)
    cs = _rope_table(positions)
    out = None
    for l in range(depth):
        x = _mixer(x, cs, batch, seq, norm_mix[l], w_in[l], b_igate[l], b_fgate[l], conv_w[l], conv_b[l],
                   mlstm_norm[l], w_branch_a[l], q_norm[l], kv_norm[l], w_uq[l], w_ukv[l],
                   w_branch_b[l], w_out[l])
        last = l == depth - 1
        if l % 2 == 0:
            h = _rmsnorm(x, norm_ffn[l], BF16)
            tm = 512
            blk_e = jnp.zeros((n // tm,), jnp.int32)
            n_used = jnp.full((1,), n // tm, jnp.int32)
            x = _ffn(h, w_gate_d[l // 2:l // 2 + 1].astype(BF16), w_up_d[l // 2:l // 2 + 1].astype(BF16),
                     w_down_d[l // 2:l // 2 + 1].astype(BF16), blk_e, n_used, x, tm=tm, tf=256)
            if last:
                out = _rmsnorm(x, norm_final, F32)
        else:
            tm = 512
            h = _rmsnorm(x, norm_ffn[l], F32)
            idx, gates = _router(h, w_router[l // 2])
            src, blk_e, n_used, pos0, pos1 = _moe_plan(idx, tm)
            xs = _gather_rows(h, src)
            ys = _ffn(xs, w_gate_e[l // 2].astype(BF16), w_up_e[l // 2].astype(BF16),
                      w_down_e[l // 2].astype(BF16), blk_e, n_used, None, tm=tm, tf=512)
            x = _combine(ys, x, gates, pos0, pos1, norm_final, normalize=last)
            if last:
                out = x
    return out.reshape(batch, seq, d)
```

```python
import functools

import jax
import jax.numpy as jnp
from jax import lax
from jax.experimental import pallas as pl
from jax.experimental.pallas import tpu as pltpu

F32 = jnp.float32
BF16 = jnp.bfloat16

EPS = 1e-6
CHUNK = 64
A_HEADS = 8
A_QK = 128
A_V = 256
B_HEADS = 16
B_NOPE = 128
B_ROPE = 64
B_V = 128
Q_LORA = 1024
KV_LORA = 512
ROPE_BASE = 10000.0
N_EXPERTS = 8
TOP_K = 2
LOG2_E = 1.4426950408889634

LANES = 128
VMEM_LIMIT = 56 * 1024 * 1024


def _cparams(sem):
    return pltpu.CompilerParams(dimension_semantics=sem, vmem_limit_bytes=VMEM_LIMIT)


def _rmsnorm_kernel(x_ref, g_ref, o_ref):
    x = x_ref[...].astype(F32)
    r = lax.rsqrt(jnp.mean(x * x, axis=-1, keepdims=True) + EPS)
    o_ref[...] = (x * r * g_ref[...]).astype(o_ref.dtype)


def _rmsnorm(x, g, out_dtype, tm=512):
    m, d = x.shape
    return pl.pallas_call(
        _rmsnorm_kernel,
        grid=(m // tm,),
        in_specs=[pl.BlockSpec((tm, d), lambda i: (i, 0)),
                  pl.BlockSpec((1, d), lambda i: (0, 0))],
        out_specs=pl.BlockSpec((tm, d), lambda i: (i, 0)),
        out_shape=jax.ShapeDtypeStruct((m, d), out_dtype),
        compiler_params=_cparams(("parallel",)),
        name="rmsnorm",
    )(x, g.reshape(1, d).astype(F32))


def _mm_kernel(a_ref, w_ref, o_ref):
    o_ref[...] = jnp.dot(a_ref[...], w_ref[...], preferred_element_type=F32).astype(o_ref.dtype)


def _mm_res_kernel(a_ref, w_ref, r_ref, o_ref):
    acc = jnp.dot(a_ref[...], w_ref[...], preferred_element_type=F32)
    o_ref[...] = (r_ref[...] + acc).astype(o_ref.dtype)


def _matmul(a, w, n_out, out_dtype, tm, tn, residual=None, name="matmul"):
    m, k = a.shape
    tm, tn = min(tm, m), min(tn, n_out)
    grid = (m // tm, n_out // tn)
    in_specs = [pl.BlockSpec((tm, k), lambda i, j: (i, 0)),
                pl.BlockSpec((k, tn), lambda i, j: (0, j))]
    args = [a, w]
    kern = _mm_kernel
    if residual is not None:
        in_specs.append(pl.BlockSpec((tm, tn), lambda i, j: (i, j)))
        args.append(residual)
        kern = _mm_res_kernel
    return pl.pallas_call(
        kern,
        grid=grid,
        in_specs=in_specs,
        out_specs=pl.BlockSpec((tm, tn), lambda i, j: (i, j)),
        out_shape=jax.ShapeDtypeStruct((m, n_out), out_dtype),
        compiler_params=_cparams(("parallel", "parallel")),
        name=name,
    )(*args)


def _sigmoid(x):
    return 0.5 * jnp.tanh(0.5 * x) + 0.5


def _mm_merge_kernel(a_ref, w_ref, ya_ref, ga_ref, gb_ref, o_ref):
    yb = jnp.dot(a_ref[...], w_ref[...], preferred_element_type=F32)
    merged = _sigmoid(ga_ref[...]) * ya_ref[...] + _sigmoid(gb_ref[...]) * yb
    o_ref[...] = merged.astype(o_ref.dtype)


def _matmul_merge(a, w, y_a, z_tail, ga_off, gb_off, tm, tn):
    m, k = a.shape
    n = w.shape[1]
    ga_blk, gb_blk = ga_off // tn, gb_off // tn
    return pl.pallas_call(
        _mm_merge_kernel,
        grid=(m // tm, n // tn),
        in_specs=[pl.BlockSpec((tm, k), lambda i, j: (i, 0)),
                  pl.BlockSpec((k, tn), lambda i, j: (0, j)),
                  pl.BlockSpec((tm, tn), lambda i, j: (i, j)),
                  pl.BlockSpec((tm, tn), lambda i, j: (i, j + ga_blk)),
                  pl.BlockSpec((tm, tn), lambda i, j: (i, j + gb_blk))],
        out_specs=pl.BlockSpec((tm, tn), lambda i, j: (i, j)),
        out_shape=jax.ShapeDtypeStruct((m, n), BF16),
        compiler_params=_cparams(("parallel", "parallel")),
        name="branch_b_merge",
    )(a, w, y_a, z_tail, z_tail)


def _rope_table_kernel(pos_ref, inv_ref, o_ref):
    ang = pos_ref[...].astype(F32) * inv_ref[...]
    lane = lax.broadcasted_iota(jnp.int32, ang.shape, 1)
    o_ref[...] = jnp.where(lane < B_ROPE, jnp.cos(ang), jnp.sin(ang))


def _rope_table(positions, tm=1024):
    n = positions.size
    inv = ROPE_BASE ** (-jnp.arange(0, B_ROPE, 2, dtype=F32) / B_ROPE)
    inv4 = jnp.tile(inv, 4).reshape(1, LANES)
    return pl.pallas_call(
        _rope_table_kernel,
        grid=(n // tm,),
        in_specs=[pl.BlockSpec((tm, 1), lambda i: (i, 0)),
                  pl.BlockSpec((1, LANES), lambda i: (0, 0))],
        out_specs=pl.BlockSpec((tm, LANES), lambda i: (i, 0)),
        out_shape=jax.ShapeDtypeStruct((n, LANES), F32),
        compiler_params=_cparams(("parallel",)),
        name="rope_table",
    )(positions.reshape(n, 1), inv4)


def _log_sigmoid(x):
    return jnp.minimum(x, 0.0) - jnp.log1p(jnp.exp(-jnp.abs(x)))


def _mlstm_kernel(zqk_ref, zv_ref, zo_ref, g_ref, cw_ref, cb_ref, gbias_ref, nw_ref, o_ref,
                  ct_ref, n_ref, m_ref, prev_ref):
    H, dk, dv, L = A_HEADS, A_QK, A_V, CHUNK
    c = pl.program_id(1)

    @pl.when(c == 0)
    def _():
        ct_ref[...] = jnp.zeros_like(ct_ref)
        n_ref[...] = jnp.zeros_like(n_ref)
        m_ref[...] = jnp.zeros_like(m_ref)
        prev_ref[...] = jnp.zeros_like(prev_ref)

    u = zqk_ref[...]
    prev = prev_ref[...]
    kc = cw_ref.shape[0]
    row = lax.broadcasted_iota(jnp.int32, u.shape, 0)
    y = u * cw_ref[kc - 1:kc, :] + cb_ref[...]
    for d in range(1, kc):
        shifted = jnp.where(row < d, pltpu.roll(prev, d, axis=0), pltpu.roll(u, d, axis=0))
        y = y + shifted * cw_ref[kc - 1 - d:kc - d, :]
    prev_ref[...] = u
    qk = y * jax.nn.sigmoid(y)

    gb = g_ref[...] + gbias_ref[...]
    gt = gb.T
    lsg = _log_sigmoid(gb)
    lsgt = _log_sigmoid(gt)

    ri = lax.broadcasted_iota(jnp.int32, (L, L), 0)
    ci = lax.broadcasted_iota(jnp.int32, (L, L), 1)
    tril = ci <= ri

    for h in range(H):
        ig_c = gb[:, h:h + 1]
        lf_c = lsg[:, H + h:H + h + 1]
        ig_r = gt[h:h + 1, :]
        lf_r = lsgt[H + h:H + h + 1, :]
        b_c = jnp.sum(jnp.where(tril, lf_r, 0.0), axis=1, keepdims=True)
        b_r = jnp.sum(jnp.where(ri <= ci, lf_c, 0.0), axis=0, keepdims=True)
        b_last = jnp.sum(lf_r, axis=1, keepdims=True)
        a_r = b_last - b_r + ig_r
        a_c = b_last - b_c + ig_c
        m_loc = jnp.max(a_r, axis=1, keepdims=True)
        wa_c = jnp.exp(a_c - m_loc)

        q = qk[:, h * dk:(h + 1) * dk]
        k = qk[:, (H + h) * dk:(H + h + 1) * dk] * (dk ** -0.5)
        v = zv_ref[:, h * dv:(h + 1) * dv]
        q_b = q.astype(BF16)
        k_b = k.astype(BF16)

        m_st = m_ref[h]
        ct = ct_ref[h]
        nvec = n_ref[h]

        dm = jnp.where(tril, b_c - b_r + ig_r, -jnp.inf)
        m_inter = b_c + m_st
        m_j = jnp.maximum(m_inter, jnp.max(dm, axis=1, keepdims=True))
        s = lax.dot_general(q_b, k_b, (((1,), (1,)), ((), ())), preferred_element_type=F32)
        qkw = s * jnp.exp(dm - m_j)
        s_inter = jnp.exp(m_inter - m_j)
        num = (jnp.dot(qkw.astype(BF16), v.astype(BF16), preferred_element_type=F32)
               + s_inter * jnp.dot(q_b, ct.astype(BF16), preferred_element_type=F32))
        den = (jnp.sum(qkw, axis=1, keepdims=True)
               + s_inter * jnp.sum(q * nvec, axis=1, keepdims=True))
        hh = num / jnp.maximum(jnp.abs(den), jnp.exp(-m_j))

        r = lax.rsqrt(jnp.mean(hh * hh, axis=-1, keepdims=True) + EPS)
        ha = hh * r * nw_ref[h]
        og = jax.nn.sigmoid(zo_ref[:, h * dv:(h + 1) * dv])
        o_ref[:, h * dv:(h + 1) * dv] = (og * ha).astype(o_ref.dtype)

        vw = (v * wa_c).astype(BF16)
        ct_loc = jnp.dot(k.T.astype(BF16), vw, preferred_element_type=F32)
        n_loc = jnp.sum(k * wa_c, axis=0, keepdims=True)
        m_new = jnp.maximum(b_last + m_st, m_loc)
        s_old = jnp.exp(b_last + m_st - m_new)
        s_loc = jnp.exp(m_loc - m_new)
        ct_ref[h] = s_old * ct + s_loc * ct_loc
        n_ref[h] = s_old * nvec + s_loc * n_loc
        m_ref[h] = m_new


def _mlstm(z_main, z_tail, gate_blk, conv_w, conv_b, gbias, norm_w, batch, seq):
    H, dk, dv, L = A_HEADS, A_QK, A_V, CHUNK
    n = batch * seq
    nc = seq // L
    qk_w, v_w = 2 * H * dk, H * dv
    assert qk_w == v_w
    row = lambda b, c: b * nc + c
    return pl.pallas_call(
        _mlstm_kernel,
        grid=(batch, nc),
        in_specs=[pl.BlockSpec((L, qk_w), lambda b, c: (row(b, c), 0)),
                  pl.BlockSpec((L, v_w), lambda b, c: (row(b, c), 1)),
                  pl.BlockSpec((L, v_w), lambda b, c: (row(b, c), 2)),
                  pl.BlockSpec((L, LANES), lambda b, c: (row(b, c), gate_blk)),
                  pl.BlockSpec(conv_w.shape, lambda b, c: (0, 0)),
                  pl.BlockSpec((1, qk_w), lambda b, c: (0, 0)),
                  pl.BlockSpec((1, LANES), lambda b, c: (0, 0)),
                  pl.BlockSpec((H, 1, dv), lambda b, c: (0, 0, 0))],
        out_specs=pl.BlockSpec((L, v_w), lambda b, c: (row(b, c), 0)),
        out_shape=jax.ShapeDtypeStruct((n, v_w), BF16),
        scratch_shapes=[pltpu.VMEM((H, dk, dv), F32),
                        pltpu.VMEM((H, 1, dk), F32),
                        pltpu.VMEM((H, 1, 1), F32),
                        pltpu.VMEM((L, qk_w), F32)],
        compiler_params=_cparams(("parallel", "arbitrary")),
        name="mlstm",
    )(z_main, z_main, z_main, z_tail, conv_w, conv_b.reshape(1, qk_w), gbias, norm_w.reshape(H, 1, dv))


def _mla_prep_kernel(cq_ref, ckv_ref, kr_ref, cs_ref, qn_ref, kvn_ref, cqn_ref, ckvn_ref, kro_ref):
    cq = cq_ref[...]
    r = lax.rsqrt(jnp.mean(cq * cq, axis=-1, keepdims=True) + EPS)
    cqn_ref[...] = (cq * r * qn_ref[...]).astype(cqn_ref.dtype)
    ckv = ckv_ref[...]
    r = lax.rsqrt(jnp.mean(ckv * ckv, axis=-1, keepdims=True) + EPS)
    ckvn_ref[...] = (ckv * r * kvn_ref[...]).astype(ckvn_ref.dtype)
    t = kr_ref[...] * cs_ref[...]
    kro_ref[...] = t + pltpu.roll(t, B_ROPE, axis=1)


def _mla_prep(z_tail, cs, q_norm, kv_norm, cq_off, ckv_off, kr_off, tm=512):
    n = z_tail.shape[0]
    return pl.pallas_call(
        _mla_prep_kernel,
        grid=(n // tm,),
        in_specs=[pl.BlockSpec((tm, Q_LORA), lambda i: (i, cq_off // Q_LORA)),
                  pl.BlockSpec((tm, KV_LORA), lambda i: (i, ckv_off // KV_LORA)),
                  pl.BlockSpec((tm, LANES), lambda i: (i, kr_off // LANES)),
                  pl.BlockSpec((tm, LANES), lambda i: (i, 0)),
                  pl.BlockSpec((1, Q_LORA), lambda i: (0, 0)),
                  pl.BlockSpec((1, KV_LORA), lambda i: (0, 0))],
        out_specs=[pl.BlockSpec((tm, Q_LORA), lambda i: (i, 0)),
                   pl.BlockSpec((tm, KV_LORA), lambda i: (i, 0)),
                   pl.BlockSpec((tm, LANES), lambda i: (i, 0))],
        out_shape=[jax.ShapeDtypeStruct((n, Q_LORA), BF16),
                   jax.ShapeDtypeStruct((n, KV_LORA), BF16),
                   jax.ShapeDtypeStruct((n, LANES), F32)],
        compiler_params=_cparams(("parallel",)),
        name="mla_prep",
    )(z_tail, z_tail, z_tail, cs, q_norm.reshape(1, Q_LORA), kv_norm.reshape(1, KV_LORA))


def _uq_kernel(a_ref, w_ref, cs_ref, o_ref):
    scale = (B_NOPE + B_ROPE) ** -0.5 * LOG2_E
    a = a_ref[...]
    cs = cs_ref[...]
    for h in range(w_ref.shape[0]):
        acc = jnp.dot(a, w_ref[h], preferred_element_type=F32)
        t = acc[:, B_NOPE:] * cs
        rot = t + pltpu.roll(t, B_ROPE, axis=1)
        o_ref[h, :, :B_NOPE] = (acc[:, :B_NOPE] * scale).astype(o_ref.dtype)
        o_ref[h, :, B_NOPE:] = (rot[:, :B_ROPE] * scale).astype(o_ref.dtype)


def _uq_proj(cqn, w_q, cs, tm=1024, hb=4):
    n = cqn.shape[0]
    dq = B_NOPE + B_ROPE
    return pl.pallas_call(
        _uq_kernel,
        grid=(n // tm, B_HEADS // hb),
        in_specs=[pl.BlockSpec((tm, Q_LORA), lambda i, h: (i, 0)),
                  pl.BlockSpec((hb, Q_LORA, 2 * LANES), lambda i, h: (h, 0, 0)),
                  pl.BlockSpec((tm, LANES), lambda i, h: (i, 0))],
        out_specs=pl.BlockSpec((hb, tm, dq), lambda i, h: (h, i, 0)),
        out_shape=jax.ShapeDtypeStruct((B_HEADS, n, dq), BF16),
        compiler_params=_cparams(("parallel", "parallel")),
        name="uq_proj",
    )(cqn, w_q, cs)


def _ukv_kernel(a_ref, w_ref, kr_ref, k_ref, v_ref):
    a = a_ref[...]
    kr = kr_ref[:, :B_ROPE].astype(k_ref.dtype)
    for h in range(w_ref.shape[0]):
        acc = jnp.dot(a, w_ref[h], preferred_element_type=F32)
        k_ref[h, :, :B_NOPE] = acc[:, :B_NOPE].astype(k_ref.dtype)
        k_ref[h, :, B_NOPE:] = kr
        v_ref[h] = acc[:, B_NOPE:].astype(v_ref.dtype)


def _ukv_proj(ckvn, w_kv, kr, tm=1024, hb=4):
    n = ckvn.shape[0]
    dq = B_NOPE + B_ROPE
    return pl.pallas_call(
        _ukv_kernel,
        grid=(n // tm, B_HEADS // hb),
        in_specs=[pl.BlockSpec((tm, KV_LORA), lambda i, h: (i, 0)),
                  pl.BlockSpec((hb, KV_LORA, B_NOPE + B_V), lambda i, h: (h, 0, 0)),
                  pl.BlockSpec((tm, LANES), lambda i, h: (i, 0))],
        out_specs=[pl.BlockSpec((hb, tm, dq), lambda i, h: (h, i, 0)),
                   pl.BlockSpec((hb, tm, B_V), lambda i, h: (h, i, 0))],
        out_shape=[jax.ShapeDtypeStruct((B_HEADS, n, dq), BF16),
                   jax.ShapeDtypeStruct((B_HEADS, n, B_V), BF16)],
        compiler_params=_cparams(("parallel", "parallel")),
        name="ukv_proj",
    )(ckvn, w_kv, kr)


def _attn_kernel(q_ref, k_ref, v_ref, o_ref, m_ref, l_ref, acc_ref, *, tq, tk, hb):
    i = pl.program_id(2)
    dv = v_ref.shape[-1]
    ncol = tk // LANES

    m_ref[...] = jnp.full(m_ref.shape, -jnp.inf, F32)
    l_ref[...] = jnp.zeros(l_ref.shape, F32)
    acc_ref[...] = jnp.zeros(acc_ref.shape, F32)

    def step(h, row0, off, masked):
        rows = pl.ds(row0, tq - row0)
        kb = k_ref[h, pl.ds(off, tk), :]
        vb = v_ref[h, pl.ds(off, tk), :]
        s = lax.dot_general(q_ref[h, rows, :], kb, (((1,), (1,)), ((), ())), preferred_element_type=F32)
        if masked:
            qc = lax.broadcasted_iota(jnp.int32, s.shape, 0) // CHUNK
            kc = lax.broadcasted_iota(jnp.int32, s.shape, 1) // CHUNK
            s = jnp.where(kc <= qc, s, -jnp.inf)
        m_prev = m_ref[h, rows, :]
        m_new = jnp.maximum(m_prev, jnp.max(s, axis=-1, keepdims=True))
        p = jnp.exp2(s - jnp.tile(m_new, (1, ncol)))
        alpha = jnp.exp2(m_prev - m_new)
        l_ref[h, rows, :] = alpha * l_ref[h, rows, :] + jnp.sum(p, axis=-1, keepdims=True)
        acc_ref[h, rows, :] = alpha * acc_ref[h, rows, :] + jnp.dot(
            p.astype(BF16), vb, preferred_element_type=F32)
        m_ref[h, rows, :] = m_new

    def body(j, carry):
        off = pl.multiple_of(j * tk, tk)
        for h in range(hb):
            step(h, 0, off, False)
        return carry

    lax.fori_loop(0, i * (tq // tk), body, 0)
    for c in range(tq // tk):
        off = pl.multiple_of(i * tq + c * tk, tk)
        for h in range(hb):
            step(h, c * tk, off, True)
    for h in range(hb):
        o_ref[:, h * dv:(h + 1) * dv] = (acc_ref[h] / l_ref[h]).astype(o_ref.dtype)


def _attention(q, k, v, batch, seq, tq=1024, tk=512, hb=2):
    nq = seq // tq
    n = batch * seq
    dq = q.shape[-1]
    assert B_V == LANES and tk % CHUNK == 0 and tq % tk == 0
    return pl.pallas_call(
        functools.partial(_attn_kernel, tq=tq, tk=tk, hb=hb),
        grid=(batch, B_HEADS // hb, nq),
        in_specs=[pl.BlockSpec((hb, tq, dq), lambda b, h, i: (h, b * nq + i, 0)),
                  pl.BlockSpec((hb, seq, dq), lambda b, h, i: (h, b, 0)),
                  pl.BlockSpec((hb, seq, B_V), lambda b, h, i: (h, b, 0))],
        out_specs=pl.BlockSpec((tq, hb * B_V), lambda b, h, i: (b * nq + i, h)),
        out_shape=jax.ShapeDtypeStruct((n, B_HEADS * B_V), BF16),
        scratch_shapes=[pltpu.VMEM((hb, tq, LANES), F32),
                        pltpu.VMEM((hb, tq, LANES), F32),
                        pltpu.VMEM((hb, tq, B_V), F32)],
        compiler_params=_cparams(("parallel", "parallel", "arbitrary")),
        name="attention",
    )(q, k, v)


def _ffn_kernel(be_ref, nu_ref, x_ref, wg_ref, wu_ref, wd_ref, *rest, has_res):
    o_ref = rest[-1]
    i = pl.program_id(0)
    f = pl.program_id(1)
    used = i < nu_ref[0]

    @pl.when(f == 0)
    def _():
        o_ref[...] = rest[0][...] if has_res else jnp.zeros_like(o_ref)

    @pl.when(used)
    def _():
        x = x_ref[...].astype(BF16)
        g = jnp.dot(x, wg_ref[0], preferred_element_type=F32)
        u = jnp.dot(x, wu_ref[0], preferred_element_type=F32)
        h1 = (g * jax.nn.sigmoid(g) * u).astype(BF16)
        o_ref[...] += jnp.dot(h1, wd_ref[0], preferred_element_type=F32)


def _ffn(x, w_gate, w_up, w_down, blk_e, n_used, residual, tm, tf):
    m, d = x.shape
    fdim = w_gate.shape[-1]
    nblk = m // tm

    def row_map(i, f, be, nu):
        return (jnp.minimum(i, nu[0] - 1), 0)

    in_specs = [pl.BlockSpec((tm, d), row_map, pipeline_mode=pl.Buffered(1)),
                pl.BlockSpec((1, d, tf), lambda i, f, be, nu: (be[i], 0, f)),
                pl.BlockSpec((1, d, tf), lambda i, f, be, nu: (be[i], 0, f)),
                pl.BlockSpec((1, tf, d), lambda i, f, be, nu: (be[i], f, 0))]
    args = [x, w_gate, w_up, w_down]
    if residual is not None:
        in_specs.append(pl.BlockSpec((tm, d), row_map, pipeline_mode=pl.Buffered(1)))
        args.append(residual)
    return pl.pallas_call(
        functools.partial(_ffn_kernel, has_res=residual is not None),
        grid_spec=pltpu.PrefetchScalarGridSpec(
            num_scalar_prefetch=2,
            grid=(nblk, fdim // tf),
            in_specs=in_specs,
            out_specs=pl.BlockSpec((tm, d), lambda i, f, be, nu: (i, 0)),
        ),
        out_shape=jax.ShapeDtypeStruct((m, d), F32),
        compiler_params=_cparams(("parallel", "arbitrary")),
        name="swiglu",
    )(blk_e, n_used, *args)


def _router_kernel(x_ref, w_ref, idx_ref, gate_ref):
    x = x_ref[...]
    w = w_ref[...]
    x_hi = x.astype(BF16)
    x_lo = (x - x_hi.astype(F32)).astype(BF16)
    w_hi = w.astype(BF16)
    w_lo = (w - w_hi.astype(F32)).astype(BF16)
    logits = (jnp.dot(x_hi, w_hi, preferred_element_type=F32)
              + jnp.dot(x_hi, w_lo, preferred_element_type=F32)
              + jnp.dot(x_lo, w_hi, preferred_element_type=F32))
    lane = lax.broadcasted_iota(jnp.int32, logits.shape, 1)
    logits = jnp.where(lane < N_EXPERTS, logits, -jnp.inf)
    m1 = jnp.max(logits, axis=-1, keepdims=True)
    i1 = jnp.min(jnp.where(logits == m1, lane, LANES), axis=-1, keepdims=True)
    rest = jnp.where(lane == i1, -jnp.inf, logits)
    m2 = jnp.max(rest, axis=-1, keepdims=True)
    i2 = jnp.min(jnp.where(rest == m2, lane, LANES), axis=-1, keepdims=True)
    e2 = jnp.exp(m2 - m1)
    g1 = 1.0 / (1.0 + e2)
    g2 = e2 / (1.0 + e2)
    idx_ref[...] = jnp.where(lane == 0, i1, jnp.where(lane == 1, i2, 0))
    gate_ref[...] = jnp.where(lane == 0, g1, jnp.where(lane == 1, g2, 0.0))


def _router(h, w_router, tm=512):
    n, d = h.shape
    w_pad = jnp.zeros((d, LANES), F32).at[:, :N_EXPERTS].set(w_router)
    idx, gate = pl.pallas_call(
        _router_kernel,
        grid=(n // tm,),
        in_specs=[pl.BlockSpec((tm, d), lambda i: (i, 0)),
                  pl.BlockSpec((d, LANES), lambda i: (0, 0))],
        out_specs=[pl.BlockSpec((tm, LANES), lambda i: (i, 0)),
                   pl.BlockSpec((tm, LANES), lambda i: (i, 0))],
        out_shape=[jax.ShapeDtypeStruct((n, LANES), jnp.int32),
                   jax.ShapeDtypeStruct((n, LANES), F32)],
        compiler_params=_cparams(("parallel",)),
        name="router",
    )(h, w_pad)
    return idx[:, :TOP_K], gate


def _gather_kernel(src_ref, x_hbm, o_ref, sem, *, tr):
    base = pl.program_id(0) * tr

    def copy(r):
        return pltpu.make_async_copy(x_hbm.at[pl.ds(src_ref[base + r], 1)], o_ref.at[pl.ds(r, 1)], sem)

    def start(r, carry):
        copy(r).start()
        return carry

    def wait(r, carry):
        copy(r).wait()
        return carry

    lax.fori_loop(0, tr, start, 0)
    lax.fori_loop(0, tr, wait, 0)


def _gather_rows(x, src, tr=256):
    p = src.shape[0]
    d = x.shape[1]
    return pl.pallas_call(
        functools.partial(_gather_kernel, tr=tr),
        grid_spec=pltpu.PrefetchScalarGridSpec(
            num_scalar_prefetch=1,
            grid=(p // tr,),
            in_specs=[pl.BlockSpec(memory_space=pl.ANY)],
            out_specs=pl.BlockSpec((tr, d), lambda i, s: (i, 0)),
            scratch_shapes=[pltpu.SemaphoreType.DMA(())],
        ),
        out_shape=jax.ShapeDtypeStruct((p, d), x.dtype),
        compiler_params=_cparams(("arbitrary",)),
        name="moe_gather",
    )(src, x)


def _combine_kernel(p0_ref, p1_ref, y_hbm, x_ref, g_ref, nw_ref, o_ref, b0_ref, b1_ref, sem, *, tc, normalize):
    base = pl.program_id(0) * tc

    def copies(r):
        return (pltpu.make_async_copy(y_hbm.at[pl.ds(p0_ref[base + r], 1)], b0_ref.at[pl.ds(r, 1)], sem.at[0]),
                pltpu.make_async_copy(y_hbm.at[pl.ds(p1_ref[base + r], 1)], b1_ref.at[pl.ds(r, 1)], sem.at[1]))

    def start(r, carry):
        c0, c1 = copies(r)
        c0.start()
        c1.start()
        return carry

    def wait(r, carry):
        c0, c1 = copies(r)
        c0.wait()
        c1.wait()
        return carry

    lax.fori_loop(0, tc, start, 0)
    lax.fori_loop(0, tc, wait, 0)
    g = g_ref[...]
    x = x_ref[...] + g[:, 0:1] * b0_ref[...] + g[:, 1:2] * b1_ref[...]
    if normalize:
        r = lax.rsqrt(jnp.mean(x * x, axis=-1, keepdims=True) + EPS)
        x = x * r * nw_ref[...]
    o_ref[...] = x


def _combine(y_sorted, x, gates, pos0, pos1, norm_w, normalize, tc=256):
    n, d = x.shape
    return pl.pallas_call(
        functools.partial(_combine_kernel, tc=tc, normalize=normalize),
        grid_spec=pltpu.PrefetchScalarGridSpec(
            num_scalar_prefetch=2,
            grid=(n // tc,),
            in_specs=[pl.BlockSpec(memory_space=pl.ANY),
                      pl.BlockSpec((tc, d), lambda i, a, b: (i, 0)),
                      pl.BlockSpec((tc, LANES), lambda i, a, b: (i, 0)),
                      pl.BlockSpec((1, d), lambda i, a, b: (0, 0))],
            out_specs=pl.BlockSpec((tc, d), lambda i, a, b: (i, 0)),
            scratch_shapes=[pltpu.VMEM((tc, d), F32), pltpu.VMEM((tc, d), F32),
                            pltpu.SemaphoreType.DMA((2,))],
        ),
        out_shape=jax.ShapeDtypeStruct((n, d), F32),
        compiler_params=_cparams(("arbitrary",)),
        name="moe_combine",
    )(pos0, pos1, y_sorted, x, gates, norm_w.reshape(1, d))


def _moe_plan(idx, tm):
    n = idx.shape[0]
    a = n * TOP_K
    e_flat = idx.reshape(a)
    onehot = (e_flat[:, None] == jnp.arange(N_EXPERTS, dtype=jnp.int32)[None, :]).astype(jnp.int32)
    csum = jnp.cumsum(onehot, axis=0)
    rank = jnp.sum(onehot * (csum - 1), axis=1)
    counts = csum[-1]
    padded = (counts + tm - 1) // tm * tm
    pend = jnp.cumsum(padded)
    pstart = pend - padded
    dest = (pstart[e_flat] + rank).astype(jnp.int32)
    p = a + N_EXPERTS * tm
    nblk = p // tm
    src = jnp.zeros((p,), jnp.int32).at[dest].set(jnp.arange(a, dtype=jnp.int32) // TOP_K)
    blk_e = jnp.minimum(jnp.searchsorted(pend, jnp.arange(nblk, dtype=jnp.int32) * tm, side="right"),
                        N_EXPERTS - 1).astype(jnp.int32)
    n_used = (pend[-1:] // tm).astype(jnp.int32)
    pos = dest.reshape(n, TOP_K)
    return src, blk_e, n_used, pos[:, 0], pos[:, 1]


def _in_proj_tail(w_in_l):
    o = 2 * A_HEADS * A_QK + 2 * A_HEADS * A_V
    a_i = w_in_l[:, o:o + A_HEADS]
    a_f = w_in_l[:, o + A_HEADS:o + 2 * A_HEADS]
    o += 2 * A_HEADS
    c_q = w_in_l[:, o:o + Q_LORA]
    o += Q_LORA
    c_kv = w_in_l[:, o:o + KV_LORA]
    o += KV_LORA
    k_r = w_in_l[:, o:o + B_ROPE]
    o += B_ROPE
    d = w_in_l.shape[0]
    g_a = w_in_l[:, o:o + d]
    g_b = w_in_l[:, o + d:o + 2 * d]
    half = B_ROPE // 2
    k_r_sw = jnp.concatenate([-k_r[:, half:], k_r[:, :half]], axis=1)
    used = 2 * d + Q_LORA + KV_LORA + 2 * B_ROPE + 2 * A_HEADS
    pad = jnp.zeros((d, -used % 512), w_in_l.dtype)
    tail = jnp.concatenate([g_a, g_b, c_q, c_kv, k_r, k_r_sw, a_i, a_f, pad], axis=1)
    offs = dict(g_a=0, g_b=d, c_q=2 * d, c_kv=2 * d + Q_LORA, k_r=2 * d + Q_LORA + KV_LORA,
                gates=2 * d + Q_LORA + KV_LORA + LANES)
    return tail.astype(BF16), offs


def _uq_weight(w_uq_l):
    w = w_uq_l.reshape(Q_LORA, B_HEADS, B_NOPE + B_ROPE)
    nope, rope = w[..., :B_NOPE], w[..., B_NOPE:]
    half = B_ROPE // 2
    rope_sw = jnp.concatenate([-rope[..., half:], rope[..., :half]], axis=-1)
    return jnp.transpose(jnp.concatenate([nope, rope, rope_sw], axis=-1), (1, 0, 2)).astype(BF16)


def _ukv_weight(w_ukv_l):
    w = w_ukv_l.reshape(KV_LORA, B_HEADS, B_NOPE + B_V)
    return jnp.transpose(w, (1, 0, 2)).astype(BF16)


def _mixer(x, cs, batch, seq, norm_w, w_in_l, b_i, b_f, conv_w, conv_b, mlstm_norm, w_a,
           q_norm, kv_norm, w_uq_l, w_ukv_l, w_b, w_out_l):
    d = x.shape[1]
    h = _rmsnorm(x, norm_w, BF16)
    main_w = 2 * A_HEADS * A_QK + 2 * A_HEADS * A_V
    z_main = _matmul(h, w_in_l[:, :main_w].astype(BF16), main_w, F32, tm=1024, tn=512, name="in_proj_main")
    w_tail, offs = _in_proj_tail(w_in_l)
    z_tail = _matmul(h, w_tail, w_tail.shape[1], F32, tm=1024, tn=512, name="in_proj_tail")

    gbias = jnp.zeros((1, LANES), F32).at[0, :A_HEADS].set(b_i).at[0, A_HEADS:2 * A_HEADS].set(b_f)
    h_a = _mlstm(z_main, z_tail, offs["gates"] // LANES, conv_w, conv_b, gbias, mlstm_norm, batch, seq)
    y_a = _matmul(h_a, w_a.astype(BF16), d, F32, tm=1024, tn=512, name="branch_a")

    cqn, ckvn, kr = _mla_prep(z_tail, cs, q_norm, kv_norm, offs["c_q"], offs["c_kv"], offs["k_r"])
    q = _uq_proj(cqn, _uq_weight(w_uq_l), cs)
    k, v = _ukv_proj(ckvn, _ukv_weight(w_ukv_l), kr)
    att = _attention(q, k, v, batch, seq)
    merged = _matmul_merge(att, w_b.astype(BF16), y_a, z_tail, offs["g_a"], offs["g_b"], tm=512, tn=512)
    return _matmul(merged, w_out_l.astype(BF16), d, F32, tm=1024, tn=512, residual=x, name="out_proj")


def kernel(x, positions, norm_mix, w_in, b_igate, b_fgate, conv_w, conv_b, mlstm_norm, w_branch_a, q_norm, kv_norm, w_uq, w_ukv, w_branch_b, w_out, norm_ffn, w_gate_d, w_up_d, w_down_d, w_router, w_gate_e, w_up_e, w_down_e, norm_final):
    batch, seq, d = x.shape
    n = batch * seq
    depth = norm_mix.shape[0]
    x = x.reshape(n, d)
    cs = _rope_table(positions)
    out = None
    for l in range(depth):
        x = _mixer(x, cs, batch, seq, norm_mix[l], w_in[l], b_igate[l], b_fgate[l], conv_w[l], conv_b[l],
                   mlstm_norm[l], w_branch_a[l], q_norm[l], kv_norm[l], w_uq[l], w_ukv[l],
                   w_branch_b[l], w_out[l])
        last = l == depth - 1
        if l % 2 == 0:
            h = _rmsnorm(x, norm_ffn[l], BF16)
            tm = 512
            blk_e = jnp.zeros((n // tm,), jnp.int32)
            n_used = jnp.full((1,), n // tm, jnp.int32)
            x = _ffn(h, w_gate_d[l // 2:l // 2 + 1].astype(BF16), w_up_d[l // 2:l // 2 + 1].astype(BF16),
                     w_down_d[l // 2:l // 2 + 1].astype(BF16), blk_e, n_used, x, tm=tm, tf=256)
            if last:
                out = _rmsnorm(x, norm_final, F32)
        else:
            tm = 512
            h = _rmsnorm(x, norm_ffn[l], F32)
            idx, gates = _router(h, w_router[l // 2])
            src, blk_e, n_used, pos0, pos1 = _moe_plan(idx, tm)
            xs = _gather_rows(h, src)
            ys = _ffn(xs, w_gate_e[l // 2].astype(BF16), w_up_e[l // 2].astype(BF16),
                      w_down_e[l // 2].astype(BF16), blk_e, n_used, None, tm=tm, tf=512)
            x = _combine(ys, x, gates, pos0, pos1, norm_final, normalize=last)
            if last:
                out = x
    return out.reshape(batch, seq, d)
```

```python
import functools

import jax
import jax.numpy as jnp
from jax import lax
from jax.experimental import pallas as pl
from jax.experimental.pallas import tpu as pltpu

F32 = jnp.float32
BF16 = jnp.bfloat16

EPS = 1e-6
CHUNK = 64
MLSTM_CHUNK = 128
A_HEADS = 8
A_QK = 128
A_V = 256
B_HEADS = 16
B_NOPE = 128
B_ROPE = 64
B_V = 128
Q_LORA = 1024
KV_LORA = 512
ROPE_BASE = 10000.0
N_EXPERTS = 8
TOP_K = 2
LOG2_E = 1.4426950408889634

LANES = 128
VMEM_LIMIT = 56 * 1024 * 1024


def _cparams(sem):
    return pltpu.CompilerParams(dimension_semantics=sem, vmem_limit_bytes=VMEM_LIMIT)


def _rmsnorm_kernel(x_ref, g_ref, o_ref):
    x = x_ref[...].astype(F32)
    r = lax.rsqrt(jnp.mean(x * x, axis=-1, keepdims=True) + EPS)
    o_ref[...] = (x * r * g_ref[...]).astype(o_ref.dtype)


def _rmsnorm(x, g, out_dtype, tm=512):
    m, d = x.shape
    return pl.pallas_call(
        _rmsnorm_kernel,
        grid=(m // tm,),
        in_specs=[pl.BlockSpec((tm, d), lambda i: (i, 0)),
                  pl.BlockSpec((1, d), lambda i: (0, 0))],
        out_specs=pl.BlockSpec((tm, d), lambda i: (i, 0)),
        out_shape=jax.ShapeDtypeStruct((m, d), out_dtype),
        compiler_params=_cparams(("parallel",)),
        name="rmsnorm",
    )(x, g.reshape(1, d).astype(F32))


def _mm_kernel(a_ref, w_ref, o_ref):
    o_ref[...] = jnp.dot(a_ref[...], w_ref[...], preferred_element_type=F32).astype(o_ref.dtype)


def _matmul(a, w, n_out, out_dtype, tm, tn, col0=0, name="matmul"):
    m, k = a.shape
    tm, tn = min(tm, m), min(tn, n_out)
    c0 = col0 // tn
    return pl.pallas_call(
        _mm_kernel,
        grid=(m // tm, n_out // tn),
        in_specs=[pl.BlockSpec((tm, k), lambda i, j: (i, 0)),
                  pl.BlockSpec((k, tn), lambda i, j: (0, j + c0))],
        out_specs=pl.BlockSpec((tm, tn), lambda i, j: (i, j)),
        out_shape=jax.ShapeDtypeStruct((m, n_out), out_dtype),
        compiler_params=_cparams(("parallel", "parallel")),
        name=name,
    )(a, w)


def _mm_wcast_kernel(a_ref, w_ref, *rest, has_res):
    o_ref, wbf_ref = rest[-2], rest[-1]

    @pl.when(pl.program_id(1) == 0)
    def _():
        wbf_ref[...] = w_ref[...].astype(BF16)

    acc = jnp.dot(a_ref[...], wbf_ref[...], preferred_element_type=F32)
    if has_res:
        acc = rest[0][...] + acc
    o_ref[...] = acc.astype(o_ref.dtype)


def _matmul_wcast(a, w, n_out, out_dtype, tm, tn, residual=None, name="matmul_wcast"):
    m, k = a.shape
    tm, tn = min(tm, m), min(tn, n_out)
    in_specs = [pl.BlockSpec((tm, k), lambda j, i: (i, 0)),
                pl.BlockSpec((k, tn), lambda j, i: (0, j))]
    args = [a, w]
    if residual is not None:
        in_specs.append(pl.BlockSpec((tm, tn), lambda j, i: (i, j)))
        args.append(residual)
    return pl.pallas_call(
        functools.partial(_mm_wcast_kernel, has_res=residual is not None),
        grid=(n_out // tn, m // tm),
        in_specs=in_specs,
        out_specs=pl.BlockSpec((tm, tn), lambda j, i: (i, j)),
        out_shape=jax.ShapeDtypeStruct((m, n_out), out_dtype),
        scratch_shapes=[pltpu.VMEM((k, tn), BF16)],
        compiler_params=_cparams(("parallel", "arbitrary")),
        name=name,
    )(*args)


def _sigmoid(x):
    return 0.5 * jnp.tanh(0.5 * x) + 0.5


def _branches_merge_kernel(ha_ref, wa_ref, att_ref, wb_ref, ga_ref, gb_ref, o_ref):
    ya = jnp.dot(ha_ref[...], wa_ref[...], preferred_element_type=F32)
    yb = jnp.dot(att_ref[...], wb_ref[...], preferred_element_type=F32)
    merged = _sigmoid(ga_ref[...].astype(F32)) * ya + _sigmoid(gb_ref[...].astype(F32)) * yb
    o_ref[...] = merged.astype(o_ref.dtype)


def _branches_merge(h_a, w_a, att, w_b, gates, tm, tn):
    m, k = h_a.shape
    n = w_a.shape[1]
    tm, tn = min(tm, m), min(tn, n)
    gb_blk = n // tn
    return pl.pallas_call(
        _branches_merge_kernel,
        grid=(m // tm, n // tn),
        in_specs=[pl.BlockSpec((tm, k), lambda i, j: (i, 0)),
                  pl.BlockSpec((k, tn), lambda i, j: (0, j)),
                  pl.BlockSpec((tm, k), lambda i, j: (i, 0)),
                  pl.BlockSpec((k, tn), lambda i, j: (0, j)),
                  pl.BlockSpec((tm, tn), lambda i, j: (i, j)),
                  pl.BlockSpec((tm, tn), lambda i, j: (i, j + gb_blk))],
        out_specs=pl.BlockSpec((tm, tn), lambda i, j: (i, j)),
        out_shape=jax.ShapeDtypeStruct((m, n), BF16),
        compiler_params=_cparams(("parallel", "parallel")),
        name="branches_merge",
    )(h_a, w_a, att, w_b, gates, gates)


def _rope_table_kernel(pos_ref, inv_ref, o_ref):
    ang = pos_ref[...].astype(F32) * inv_ref[...]
    lane = lax.broadcasted_iota(jnp.int32, ang.shape, 1)
    o_ref[...] = jnp.where(lane < B_ROPE, jnp.cos(ang), jnp.sin(ang))


def _rope_table(positions, tm=1024):
    n = positions.size
    inv = ROPE_BASE ** (-jnp.arange(0, B_ROPE, 2, dtype=F32) / B_ROPE)
    inv4 = jnp.tile(inv, 4).reshape(1, LANES)
    return pl.pallas_call(
        _rope_table_kernel,
        grid=(n // tm,),
        in_specs=[pl.BlockSpec((tm, 1), lambda i: (i, 0)),
                  pl.BlockSpec((1, LANES), lambda i: (0, 0))],
        out_specs=pl.BlockSpec((tm, LANES), lambda i: (i, 0)),
        out_shape=jax.ShapeDtypeStruct((n, LANES), F32),
        compiler_params=_cparams(("parallel",)),
        name="rope_table",
    )(positions.reshape(n, 1), inv4)


def _log_sigmoid(x):
    return jnp.minimum(x, 0.0) - jnp.log1p(jnp.exp(-jnp.abs(x)))


def _mlstm_kernel(zqk_ref, zv_ref, zo_ref, g_ref, cw_ref, cb_ref, gbias_ref, nw_ref, o_ref,
                  ct_ref, n_ref, m_ref, prev_ref):
    H, dk, dv, L = A_HEADS, A_QK, A_V, MLSTM_CHUNK
    c = pl.program_id(1)

    @pl.when(c == 0)
    def _():
        ct_ref[...] = jnp.zeros_like(ct_ref)
        n_ref[...] = jnp.zeros_like(n_ref)
        m_ref[...] = jnp.zeros_like(m_ref)
        prev_ref[...] = jnp.zeros_like(prev_ref)

    u = zqk_ref[...]
    prev = prev_ref[...]
    kc = cw_ref.shape[0]
    row = lax.broadcasted_iota(jnp.int32, u.shape, 0)
    y = u * cw_ref[kc - 1:kc, :] + cb_ref[...]
    for d in range(1, kc):
        shifted = jnp.where(row < d, pltpu.roll(prev, d, axis=0), pltpu.roll(u, d, axis=0))
        y = y + shifted * cw_ref[kc - 1 - d:kc - d, :]
    prev_ref[...] = u
    qk = y * jax.nn.sigmoid(y)

    gb = g_ref[...] + gbias_ref[...]
    gt = gb.T
    lsg = _log_sigmoid(gb)
    lsgt = _log_sigmoid(gt)

    ri = lax.broadcasted_iota(jnp.int32, (L, L), 0)
    ci = lax.broadcasted_iota(jnp.int32, (L, L), 1)
    tril = ci <= ri

    gate = []
    for h in range(H):
        ig_c = gb[:, h:h + 1]
        lf_c = lsg[:, H + h:H + h + 1]
        ig_r = gt[h:h + 1, :]
        lf_r = lsgt[H + h:H + h + 1, :]
        b_c = jnp.sum(jnp.where(tril, lf_r, 0.0), axis=1, keepdims=True)
        b_r = jnp.sum(jnp.where(ri <= ci, lf_c, 0.0), axis=0, keepdims=True)
        b_last = jnp.sum(lf_r, axis=1, keepdims=True)
        a_r = b_last - b_r + ig_r
        a_c = b_last - b_c + ig_c
        m_loc = jnp.max(a_r, axis=1, keepdims=True)
        wa_c = jnp.exp(a_c - m_loc)
        m_st = m_ref[h]
        dm = jnp.where(tril, b_c - b_r + ig_r, -jnp.inf)
        m_inter = b_c + m_st
        m_j = jnp.maximum(m_inter, jnp.max(dm, axis=1, keepdims=True))
        m_new = jnp.maximum(b_last + m_st, m_loc)
        gate.append(dict(wa_c=wa_c, decay=jnp.exp(dm - m_j), s_inter=jnp.exp(m_inter - m_j),
                         floor=jnp.exp(-m_j), m_new=m_new,
                         s_old=jnp.exp(b_last + m_st - m_new), s_loc=jnp.exp(m_loc - m_new)))

    state = []
    for h in range(H):
        t = gate[h]
        q = qk[:, h * dk:(h + 1) * dk]
        k = qk[:, (H + h) * dk:(H + h + 1) * dk] * (dk ** -0.5)
        v = zv_ref[:, h * dv:(h + 1) * dv]
        q_b = q.astype(BF16)
        k_b = k.astype(BF16)
        ct = ct_ref[h]
        nvec = n_ref[h]

        s = lax.dot_general(q_b, k_b, (((1,), (1,)), ((), ())), preferred_element_type=F32)
        qkw = s * t["decay"]
        num = (jnp.dot(qkw.astype(BF16), v.astype(BF16), preferred_element_type=F32)
               + t["s_inter"] * jnp.dot(q_b, ct.astype(BF16), preferred_element_type=F32))
        den = (jnp.sum(qkw, axis=1, keepdims=True)
               + t["s_inter"] * jnp.sum(q * nvec, axis=1, keepdims=True))
        hh = num / jnp.maximum(jnp.abs(den), t["floor"])

        r = lax.rsqrt(jnp.mean(hh * hh, axis=-1, keepdims=True) + EPS)
        ha = hh * r * nw_ref[h]
        og = jax.nn.sigmoid(zo_ref[:, h * dv:(h + 1) * dv])
        o_ref[:, h * dv:(h + 1) * dv] = (og * ha).astype(o_ref.dtype)

        vw = (v * t["wa_c"]).astype(BF16)
        ct_loc = jnp.dot(k.T.astype(BF16), vw, preferred_element_type=F32)
        n_loc = jnp.sum(k * t["wa_c"], axis=0, keepdims=True)
        state.append((t["s_old"] * ct + t["s_loc"] * ct_loc, t["s_old"] * nvec + t["s_loc"] * n_loc))

    for h in range(H):
        ct_ref[h], n_ref[h] = state[h]
        m_ref[h] = gate[h]["m_new"]


def _mlstm(z_main, z_tail, gate_blk, conv_w, conv_b, gbias, norm_w, batch, seq):
    H, dk, dv, L = A_HEADS, A_QK, A_V, MLSTM_CHUNK
    n = batch * seq
    nc = seq // L
    qk_w, v_w = 2 * H * dk, H * dv
    assert qk_w == v_w
    row = lambda b, c: b * nc + c
    return pl.pallas_call(
        _mlstm_kernel,
        grid=(batch, nc),
        in_specs=[pl.BlockSpec((L, qk_w), lambda b, c: (row(b, c), 0)),
                  pl.BlockSpec((L, v_w), lambda b, c: (row(b, c), 1)),
                  pl.BlockSpec((L, v_w), lambda b, c: (row(b, c), 2)),
                  pl.BlockSpec((L, LANES), lambda b, c: (row(b, c), gate_blk)),
                  pl.BlockSpec(conv_w.shape, lambda b, c: (0, 0)),
                  pl.BlockSpec((1, qk_w), lambda b, c: (0, 0)),
                  pl.BlockSpec((1, LANES), lambda b, c: (0, 0)),
                  pl.BlockSpec((H, 1, dv), lambda b, c: (0, 0, 0))],
        out_specs=pl.BlockSpec((L, v_w), lambda b, c: (row(b, c), 0)),
        out_shape=jax.ShapeDtypeStruct((n, v_w), BF16),
        scratch_shapes=[pltpu.VMEM((H, dk, dv), F32),
                        pltpu.VMEM((H, 1, dk), F32),
                        pltpu.VMEM((H, 1, 1), F32),
                        pltpu.VMEM((L, qk_w), F32)],
        compiler_params=_cparams(("parallel", "arbitrary")),
        name="mlstm",
    )(z_main, z_main, z_main, z_tail, conv_w, conv_b.reshape(1, qk_w), gbias, norm_w.reshape(H, 1, dv))


def _mla_prep_kernel(cq_ref, ckv_ref, kr_ref, cs_ref, qn_ref, kvn_ref, cqn_ref, ckvn_ref, kro_ref):
    cq = cq_ref[...]
    r = lax.rsqrt(jnp.mean(cq * cq, axis=-1, keepdims=True) + EPS)
    cqn_ref[...] = (cq * r * qn_ref[...]).astype(cqn_ref.dtype)
    ckv = ckv_ref[...]
    r = lax.rsqrt(jnp.mean(ckv * ckv, axis=-1, keepdims=True) + EPS)
    ckvn_ref[...] = (ckv * r * kvn_ref[...]).astype(ckvn_ref.dtype)
    t = kr_ref[...] * cs_ref[...]
    kro_ref[...] = t + pltpu.roll(t, B_ROPE, axis=1)


def _mla_prep(z_tail, cs, q_norm, kv_norm, cq_off, ckv_off, kr_off, tm=512):
    n = z_tail.shape[0]
    return pl.pallas_call(
        _mla_prep_kernel,
        grid=(n // tm,),
        in_specs=[pl.BlockSpec((tm, Q_LORA), lambda i: (i, cq_off // Q_LORA)),
                  pl.BlockSpec((tm, KV_LORA), lambda i: (i, ckv_off // KV_LORA)),
                  pl.BlockSpec((tm, LANES), lambda i: (i, kr_off // LANES)),
                  pl.BlockSpec((tm, LANES), lambda i: (i, 0)),
                  pl.BlockSpec((1, Q_LORA), lambda i: (0, 0)),
                  pl.BlockSpec((1, KV_LORA), lambda i: (0, 0))],
        out_specs=[pl.BlockSpec((tm, Q_LORA), lambda i: (i, 0)),
                   pl.BlockSpec((tm, KV_LORA), lambda i: (i, 0)),
                   pl.BlockSpec((tm, LANES), lambda i: (i, 0))],
        out_shape=[jax.ShapeDtypeStruct((n, Q_LORA), BF16),
                   jax.ShapeDtypeStruct((n, KV_LORA), BF16),
                   jax.ShapeDtypeStruct((n, LANES), F32)],
        compiler_params=_cparams(("parallel",)),
        name="mla_prep",
    )(z_tail, z_tail, z_tail, cs, q_norm.reshape(1, Q_LORA), kv_norm.reshape(1, KV_LORA))


def _uq_kernel(a_ref, w_ref, cs_ref, o_ref):
    scale = (B_NOPE + B_ROPE) ** -0.5 * LOG2_E
    a = a_ref[...]
    cs = cs_ref[...]
    for h in range(w_ref.shape[0]):
        acc = jnp.dot(a, w_ref[h], preferred_element_type=F32)
        t = acc[:, B_NOPE:] * cs
        rot = t + pltpu.roll(t, B_ROPE, axis=1)
        o_ref[h, :, :B_NOPE] = (acc[:, :B_NOPE] * scale).astype(o_ref.dtype)
        o_ref[h, :, B_NOPE:] = (rot[:, :B_ROPE] * scale).astype(o_ref.dtype)


def _uq_proj(cqn, w_q, cs, tm=1024, hb=4):
    n = cqn.shape[0]
    dq = B_NOPE + B_ROPE
    return pl.pallas_call(
        _uq_kernel,
        grid=(n // tm, B_HEADS // hb),
        in_specs=[pl.BlockSpec((tm, Q_LORA), lambda i, h: (i, 0)),
                  pl.BlockSpec((hb, Q_LORA, 2 * LANES), lambda i, h: (h, 0, 0)),
                  pl.BlockSpec((tm, LANES), lambda i, h: (i, 0))],
        out_specs=pl.BlockSpec((hb, tm, dq), lambda i, h: (h, i, 0)),
        out_shape=jax.ShapeDtypeStruct((B_HEADS, n, dq), BF16),
        compiler_params=_cparams(("parallel", "parallel")),
        name="uq_proj",
    )(cqn, w_q, cs)


def _ukv_kernel(a_ref, w_ref, kr_ref, k_ref, v_ref):
    a = a_ref[...]
    kr = kr_ref[:, :B_ROPE].astype(k_ref.dtype)
    for h in range(w_ref.shape[0]):
        acc = jnp.dot(a, w_ref[h], preferred_element_type=F32)
        k_ref[h, :, :B_NOPE] = acc[:, :B_NOPE].astype(k_ref.dtype)
        k_ref[h, :, B_NOPE:] = kr
        v_ref[h] = acc[:, B_NOPE:].astype(v_ref.dtype)


def _ukv_proj(ckvn, w_kv, kr, tm=1024, hb=4):
    n = ckvn.shape[0]
    dq = B_NOPE + B_ROPE
    return pl.pallas_call(
        _ukv_kernel,
        grid=(n // tm, B_HEADS // hb),
        in_specs=[pl.BlockSpec((tm, KV_LORA), lambda i, h: (i, 0)),
                  pl.BlockSpec((hb, KV_LORA, B_NOPE + B_V), lambda i, h: (h, 0, 0)),
                  pl.BlockSpec((tm, LANES), lambda i, h: (i, 0))],
        out_specs=[pl.BlockSpec((hb, tm, dq), lambda i, h: (h, i, 0)),
                   pl.BlockSpec((hb, tm, B_V), lambda i, h: (h, i, 0))],
        out_shape=[jax.ShapeDtypeStruct((B_HEADS, n, dq), BF16),
                   jax.ShapeDtypeStruct((B_HEADS, n, B_V), BF16)],
        compiler_params=_cparams(("parallel", "parallel")),
        name="ukv_proj",
    )(ckvn, w_kv, kr)


def _attn_kernel(q_ref, k_ref, v_ref, o_ref, m_ref, l_ref, acc_ref, *, tq, tk, hb):
    i = pl.program_id(2)
    dv = v_ref.shape[-1]
    ncol = tk // LANES

    m_ref[...] = jnp.full(m_ref.shape, -jnp.inf, F32)
    l_ref[...] = jnp.zeros(l_ref.shape, F32)
    acc_ref[...] = jnp.zeros(acc_ref.shape, F32)

    def step(h, row0, off, masked):
        rows = pl.ds(row0, tq - row0)
        kb = k_ref[h, pl.ds(off, tk), :]
        vb = v_ref[h, pl.ds(off, tk), :]
        s = lax.dot_general(q_ref[h, rows, :], kb, (((1,), (1,)), ((), ())), preferred_element_type=F32)
        if masked:
            qc = lax.broadcasted_iota(jnp.int32, s.shape, 0) // CHUNK
            kc = lax.broadcasted_iota(jnp.int32, s.shape, 1) // CHUNK
            s = jnp.where(kc <= qc, s, -jnp.inf)
        m_prev = m_ref[h, rows, :]
        m_new = jnp.maximum(m_prev, jnp.max(s, axis=-1, keepdims=True))
        p = jnp.exp2(s - jnp.tile(m_new, (1, ncol)))
        alpha = jnp.exp2(m_prev - m_new)
        l_ref[h, rows, :] = alpha * l_ref[h, rows, :] + jnp.sum(p, axis=-1, keepdims=True)
        acc_ref[h, rows, :] = alpha * acc_ref[h, rows, :] + jnp.dot(
            p.astype(BF16), vb, preferred_element_type=F32)
        m_ref[h, rows, :] = m_new

    def body(j, carry):
        off = pl.multiple_of(j * tk, tk)
        for h in range(hb):
            step(h, 0, off, False)
        return carry

    lax.fori_loop(0, i * (tq // tk), body, 0)
    for c in range(tq // tk):
        off = pl.multiple_of(i * tq + c * tk, tk)
        for h in range(hb):
            step(h, c * tk, off, True)
    for h in range(hb):
        o_ref[:, h * dv:(h + 1) * dv] = (acc_ref[h] / l_ref[h]).astype(o_ref.dtype)


def _attention(q, k, v, batch, seq, tq=1024, tk=512, hb=2):
    nq = seq // tq
    n = batch * seq
    dq = q.shape[-1]
    assert B_V == LANES and tk % CHUNK == 0 and tq % tk == 0
    return pl.pallas_call(
        functools.partial(_attn_kernel, tq=tq, tk=tk, hb=hb),
        grid=(batch, B_HEADS // hb, nq),
        in_specs=[pl.BlockSpec((hb, tq, dq), lambda b, h, i: (h, b * nq + i, 0)),
                  pl.BlockSpec((hb, seq, dq), lambda b, h, i: (h, b, 0)),
                  pl.BlockSpec((hb, seq, B_V), lambda b, h, i: (h, b, 0))],
        out_specs=pl.BlockSpec((tq, hb * B_V), lambda b, h, i: (b * nq + i, h)),
        out_shape=jax.ShapeDtypeStruct((n, B_HEADS * B_V), BF16),
        scratch_shapes=[pltpu.VMEM((hb, tq, LANES), F32),
                        pltpu.VMEM((hb, tq, LANES), F32),
                        pltpu.VMEM((hb, tq, B_V), F32)],
        compiler_params=_cparams(("parallel", "parallel", "arbitrary")),
        name="attention",
    )(q, k, v)


def _ffn_kernel(be_ref, nu_ref, x_ref, wg_ref, wu_ref, wd_ref, *rest, has_res):
    o_ref = rest[-1]
    i = pl.program_id(0)
    f = pl.program_id(1)
    used = i < nu_ref[0]

    @pl.when(f == 0)
    def _():
        o_ref[...] = rest[0][...] if has_res else jnp.zeros_like(o_ref)

    @pl.when(used)
    def _():
        x = x_ref[...].astype(BF16)
        g = jnp.dot(x, wg_ref[0], preferred_element_type=F32)
        u = jnp.dot(x, wu_ref[0], preferred_element_type=F32)
        h1 = (g * jax.nn.sigmoid(g) * u).astype(BF16)
        o_ref[...] += jnp.dot(h1, wd_ref[0], preferred_element_type=F32)


def _ffn(x, w_gate, w_up, w_down, blk_e, n_used, residual, tm, tf):
    m, d = x.shape
    fdim = w_gate.shape[-1]
    nblk = m // tm

    def row_map(i, f, be, nu):
        return (jnp.minimum(i, nu[0] - 1), 0)

    in_specs = [pl.BlockSpec((tm, d), row_map, pipeline_mode=pl.Buffered(1)),
                pl.BlockSpec((1, d, tf), lambda i, f, be, nu: (be[i], 0, f)),
                pl.BlockSpec((1, d, tf), lambda i, f, be, nu: (be[i], 0, f)),
                pl.BlockSpec((1, tf, d), lambda i, f, be, nu: (be[i], f, 0))]
    args = [x, w_gate, w_up, w_down]
    if residual is not None:
        in_specs.append(pl.BlockSpec((tm, d), row_map, pipeline_mode=pl.Buffered(1)))
        args.append(residual)
    return pl.pallas_call(
        functools.partial(_ffn_kernel, has_res=residual is not None),
        grid_spec=pltpu.PrefetchScalarGridSpec(
            num_scalar_prefetch=2,
            grid=(nblk, fdim // tf),
            in_specs=in_specs,
            out_specs=pl.BlockSpec((tm, d), lambda i, f, be, nu: (i, 0)),
        ),
        out_shape=jax.ShapeDtypeStruct((m, d), F32),
        compiler_params=_cparams(("parallel", "arbitrary")),
        name="swiglu",
    )(blk_e, n_used, *args)


def _router_kernel(x_ref, w_ref, idx_ref, gate_ref):
    x = x_ref[...]
    w = w_ref[...]
    x_hi = x.astype(BF16)
    x_lo = (x - x_hi.astype(F32)).astype(BF16)
    w_hi = w.astype(BF16)
    w_lo = (w - w_hi.astype(F32)).astype(BF16)
    logits = (jnp.dot(x_hi, w_hi, preferred_element_type=F32)
              + jnp.dot(x_hi, w_lo, preferred_element_type=F32)
              + jnp.dot(x_lo, w_hi, preferred_element_type=F32))
    lane = lax.broadcasted_iota(jnp.int32, logits.shape, 1)
    logits = jnp.where(lane < N_EXPERTS, logits, -jnp.inf)
    m1 = jnp.max(logits, axis=-1, keepdims=True)
    i1 = jnp.min(jnp.where(logits == m1, lane, LANES), axis=-1, keepdims=True)
    rest = jnp.where(lane == i1, -jnp.inf, logits)
    m2 = jnp.max(rest, axis=-1, keepdims=True)
    i2 = jnp.min(jnp.where(rest == m2, lane, LANES), axis=-1, keepdims=True)
    e2 = jnp.exp(m2 - m1)
    g1 = 1.0 / (1.0 + e2)
    g2 = e2 / (1.0 + e2)
    idx_ref[...] = jnp.where(lane == 0, i1, jnp.where(lane == 1, i2, 0))
    gate_ref[...] = jnp.where(lane == 0, g1, jnp.where(lane == 1, g2, 0.0))


def _router(h, w_router, tm=512):
    n, d = h.shape
    w_pad = jnp.zeros((d, LANES), F32).at[:, :N_EXPERTS].set(w_router)
    idx, gate = pl.pallas_call(
        _router_kernel,
        grid=(n // tm,),
        in_specs=[pl.BlockSpec((tm, d), lambda i: (i, 0)),
                  pl.BlockSpec((d, LANES), lambda i: (0, 0))],
        out_specs=[pl.BlockSpec((tm, LANES), lambda i: (i, 0)),
                   pl.BlockSpec((tm, LANES), lambda i: (i, 0))],
        out_shape=[jax.ShapeDtypeStruct((n, LANES), jnp.int32),
                   jax.ShapeDtypeStruct((n, LANES), F32)],
        compiler_params=_cparams(("parallel",)),
        name="router",
    )(h, w_pad)
    return idx[:, :TOP_K], gate


def _gather_kernel(src_ref, x_hbm, o_ref, sem, *, tr):
    base = pl.program_id(0) * tr

    def copy(r):
        return pltpu.make_async_copy(x_hbm.at[pl.ds(src_ref[base + r], 1)], o_ref.at[pl.ds(r, 1)], sem)

    def start(r, carry):
        copy(2 * r).start(priority=0)
        copy(2 * r + 1).start(priority=1)
        return carry

    def wait(r, carry):
        copy(r).wait()
        return carry

    lax.fori_loop(0, tr // 2, start, 0)
    lax.fori_loop(0, tr, wait, 0)


def _gather_rows(x, src, tr=256):
    p = src.shape[0]
    d = x.shape[1]
    return pl.pallas_call(
        functools.partial(_gather_kernel, tr=tr),
        grid_spec=pltpu.PrefetchScalarGridSpec(
            num_scalar_prefetch=1,
            grid=(p // tr,),
            in_specs=[pl.BlockSpec(memory_space=pl.ANY)],
            out_specs=pl.BlockSpec((tr, d), lambda i, s: (i, 0)),
            scratch_shapes=[pltpu.SemaphoreType.DMA(())],
        ),
        out_shape=jax.ShapeDtypeStruct((p, d), x.dtype),
        compiler_params=_cparams(("arbitrary",)),
        name="moe_gather",
    )(src, x)


def _combine_kernel(p0_ref, p1_ref, y_hbm, x_ref, g_ref, nw_ref, o_ref, b0_ref, b1_ref, sem, *, tc, normalize):
    base = pl.program_id(0) * tc

    def copies(r):
        return (pltpu.make_async_copy(y_hbm.at[pl.ds(p0_ref[base + r], 1)], b0_ref.at[pl.ds(r, 1)], sem.at[0]),
                pltpu.make_async_copy(y_hbm.at[pl.ds(p1_ref[base + r], 1)], b1_ref.at[pl.ds(r, 1)], sem.at[1]))

    def start(r, carry):
        c0, c1 = copies(r)
        c0.start(priority=0)
        c1.start(priority=1)
        return carry

    def wait(r, carry):
        c0, c1 = copies(r)
        c0.wait()
        c1.wait()
        return carry

    lax.fori_loop(0, tc, start, 0)
    lax.fori_loop(0, tc, wait, 0)
    g = g_ref[...]
    x = x_ref[...] + g[:, 0:1] * b0_ref[...] + g[:, 1:2] * b1_ref[...]
    if normalize:
        r = lax.rsqrt(jnp.mean(x * x, axis=-1, keepdims=True) + EPS)
        x = x * r * nw_ref[...]
    o_ref[...] = x


def _combine(y_sorted, x, gates, pos0, pos1, norm_w, normalize, tc=256):
    n, d = x.shape
    return pl.pallas_call(
        functools.partial(_combine_kernel, tc=tc, normalize=normalize),
        grid_spec=pltpu.PrefetchScalarGridSpec(
            num_scalar_prefetch=2,
            grid=(n // tc,),
            in_specs=[pl.BlockSpec(memory_space=pl.ANY),
                      pl.BlockSpec((tc, d), lambda i, a, b: (i, 0)),
                      pl.BlockSpec((tc, LANES), lambda i, a, b: (i, 0)),
                      pl.BlockSpec((1, d), lambda i, a, b: (0, 0))],
            out_specs=pl.BlockSpec((tc, d), lambda i, a, b: (i, 0)),
            scratch_shapes=[pltpu.VMEM((tc, d), F32), pltpu.VMEM((tc, d), F32),
                            pltpu.SemaphoreType.DMA((2,))],
        ),
        out_shape=jax.ShapeDtypeStruct((n, d), F32),
        compiler_params=_cparams(("arbitrary",)),
        name="moe_combine",
    )(pos0, pos1, y_sorted, x, gates, norm_w.reshape(1, d))


def _moe_plan(idx, tm):
    n = idx.shape[0]
    a = n * TOP_K
    e_flat = idx.reshape(a)
    onehot = (e_flat[:, None] == jnp.arange(N_EXPERTS, dtype=jnp.int32)[None, :]).astype(jnp.int32)
    csum = jnp.cumsum(onehot, axis=0)
    rank = jnp.sum(onehot * (csum - 1), axis=1)
    counts = csum[-1]
    padded = (counts + tm - 1) // tm * tm
    pend = jnp.cumsum(padded)
    pstart = pend - padded
    dest = (pstart[e_flat] + rank).astype(jnp.int32)
    p = a + N_EXPERTS * tm
    nblk = p // tm
    src = jnp.zeros((p,), jnp.int32).at[dest].set(jnp.arange(a, dtype=jnp.int32) // TOP_K)
    blk_e = jnp.minimum(jnp.searchsorted(pend, jnp.arange(nblk, dtype=jnp.int32) * tm, side="right"),
                        N_EXPERTS - 1).astype(jnp.int32)
    n_used = (pend[-1:] // tm).astype(jnp.int32)
    pos = dest.reshape(n, TOP_K)
    return src, blk_e, n_used, pos[:, 0], pos[:, 1]


def _in_proj_tail(w_in_l):
    o = 2 * A_HEADS * A_QK + 2 * A_HEADS * A_V
    a_i = w_in_l[:, o:o + A_HEADS]
    a_f = w_in_l[:, o + A_HEADS:o + 2 * A_HEADS]
    o += 2 * A_HEADS
    c_q = w_in_l[:, o:o + Q_LORA]
    o += Q_LORA
    c_kv = w_in_l[:, o:o + KV_LORA]
    o += KV_LORA
    k_r = w_in_l[:, o:o + B_ROPE]
    o += B_ROPE
    d = w_in_l.shape[0]
    g_a = w_in_l[:, o:o + d]
    g_b = w_in_l[:, o + d:o + 2 * d]
    half = B_ROPE // 2
    k_r_sw = jnp.concatenate([-k_r[:, half:], k_r[:, :half]], axis=1)
    used = 2 * d + Q_LORA + KV_LORA + 2 * B_ROPE + 2 * A_HEADS
    pad = jnp.zeros((d, -used % 512), w_in_l.dtype)
    tail = jnp.concatenate([g_a, g_b, c_q, c_kv, k_r, k_r_sw, a_i, a_f, pad], axis=1)
    offs = dict(c_q=0, c_kv=Q_LORA, k_r=Q_LORA + KV_LORA, gates=Q_LORA + KV_LORA + LANES)
    return tail.astype(BF16), offs


def _uq_weight(w_uq_l):
    w = w_uq_l.reshape(Q_LORA, B_HEADS, B_NOPE + B_ROPE)
    nope, rope = w[..., :B_NOPE], w[..., B_NOPE:]
    half = B_ROPE // 2
    rope_sw = jnp.concatenate([-rope[..., half:], rope[..., :half]], axis=-1)
    return jnp.transpose(jnp.concatenate([nope, rope, rope_sw], axis=-1), (1, 0, 2)).astype(BF16)


def _ukv_weight(w_ukv_l):
    w = w_ukv_l.reshape(KV_LORA, B_HEADS, B_NOPE + B_V)
    return jnp.transpose(w, (1, 0, 2)).astype(BF16)


def _mixer(x, cs, batch, seq, norm_w, w_in_l, b_i, b_f, conv_w, conv_b, mlstm_norm, w_a,
           q_norm, kv_norm, w_uq_l, w_ukv_l, w_b, w_out_l):
    d = x.shape[1]
    h = _rmsnorm(x, norm_w, BF16)
    main_w = 2 * A_HEADS * A_QK + 2 * A_HEADS * A_V
    z_main = _matmul_wcast(h, w_in_l, main_w, F32, tm=1024, tn=512, name="in_proj_main")
    w_tail, offs = _in_proj_tail(w_in_l)
    gates = _matmul(h, w_tail, 2 * d, BF16, tm=1024, tn=512, name="in_proj_gates")
    z_rest = _matmul(h, w_tail, w_tail.shape[1] - 2 * d, F32, tm=1024, tn=512, col0=2 * d, name="in_proj_rest")

    gbias = jnp.zeros((1, LANES), F32).at[0, :A_HEADS].set(b_i).at[0, A_HEADS:2 * A_HEADS].set(b_f)
    h_a = _mlstm(z_main, z_rest, offs["gates"] // LANES, conv_w, conv_b, gbias, mlstm_norm, batch, seq)

    cqn, ckvn, kr = _mla_prep(z_rest, cs, q_norm, kv_norm, offs["c_q"], offs["c_kv"], offs["k_r"])
    q = _uq_proj(cqn, _uq_weight(w_uq_l), cs)
    k, v = _ukv_proj(ckvn, _ukv_weight(w_ukv_l), kr)
    att = _attention(q, k, v, batch, seq)
    merged = _branches_merge(h_a, w_a.astype(BF16), att, w_b.astype(BF16), gates, tm=1024, tn=512)
    return _matmul_wcast(merged, w_out_l, d, F32, tm=1024, tn=512, residual=x, name="out_proj")


def kernel(x, positions, norm_mix, w_in, b_igate, b_fgate, conv_w, conv_b, mlstm_norm, w_branch_a, q_norm, kv_norm, w_uq, w_ukv, w_branch_b, w_out, norm_ffn, w_gate_d, w_up_d, w_down_d, w_router, w_gate_e, w_up_e, w_down_e, norm_final):
    batch, seq, d = x.shape
    n = batch * seq
    depth = norm_mix.shape[0]
    x = x.reshape(n, d)
    cs = _rope_table(positions)
    out = None
    for l in range(depth):
        x = _mixer(x, cs, batch, seq, norm_mix[l], w_in[l], b_igate[l], b_fgate[l], conv_w[l], conv_b[l],
                   mlstm_norm[l], w_branch_a[l], q_norm[l], kv_norm[l], w_uq[l], w_ukv[l],
                   w_branch_b[l], w_out[l])
        last = l == depth - 1
        if l % 2 == 0:
            h = _rmsnorm(x, norm_ffn[l], BF16)
            tm = 512
            blk_e = jnp.zeros((n // tm,), jnp.int32)
            n_used = jnp.full((1,), n // tm, jnp.int32)
            x = _ffn(h, w_gate_d[l // 2:l // 2 + 1].astype(BF16), w_up_d[l // 2:l // 2 + 1].astype(BF16),
                     w_down_d[l // 2:l // 2 + 1].astype(BF16), blk_e, n_used, x, tm=tm, tf=256)
            if last:
                out = _rmsnorm(x, norm_final, F32)
        else:
            tm = 512
            h = _rmsnorm(x, norm_ffn[l], F32)
            idx, gates = _router(h, w_router[l // 2])
            src, blk_e, n_used, pos0, pos1 = _moe_plan(idx, tm)
            xs = _gather_rows(h, src)
            ys = _ffn(xs, w_gate_e[l // 2].astype(BF16), w_up_e[l // 2].astype(BF16),
                      w_down_e[l // 2].astype(BF16), blk_e, n_used, None, tm=tm, tf=512)
            x = _combine(ys, x, gates, pos0, pos1, norm_final, normalize=last)
            if last:
                out = x
    return out.reshape(batch, seq, d)
```

```python
import functools

import jax
import jax.numpy as jnp
from jax import lax
from jax.experimental import pallas as pl
from jax.experimental.pallas import tpu as pltpu

F32 = jnp.float32
BF16 = jnp.bfloat16

EPS = 1e-6
CHUNK = 64
MLSTM_CHUNK = 128
A_HEADS = 8
A_QK = 128
A_V = 256
B_HEADS = 16
B_NOPE = 128
B_ROPE = 64
B_V = 128
Q_LORA = 1024
KV_LORA = 512
ROPE_BASE = 10000.0
N_EXPERTS = 8
TOP_K = 2
LOG2_E = 1.4426950408889634

LANES = 128
VMEM_LIMIT = 56 * 1024 * 1024


def _cparams(sem):
    return pltpu.CompilerParams(dimension_semantics=sem, vmem_limit_bytes=VMEM_LIMIT)


def _rmsnorm_kernel(x_ref, g_ref, o_ref):
    x = x_ref[...].astype(F32)
    r = lax.rsqrt(jnp.mean(x * x, axis=-1, keepdims=True) + EPS)
    o_ref[...] = (x * r * g_ref[...]).astype(o_ref.dtype)


def _rmsnorm(x, g, out_dtype, tm=512):
    m, d = x.shape
    return pl.pallas_call(
        _rmsnorm_kernel,
        grid=(m // tm,),
        in_specs=[pl.BlockSpec((tm, d), lambda i: (i, 0)),
                  pl.BlockSpec((1, d), lambda i: (0, 0))],
        out_specs=pl.BlockSpec((tm, d), lambda i: (i, 0)),
        out_shape=jax.ShapeDtypeStruct((m, d), out_dtype),
        compiler_params=_cparams(("parallel",)),
        name="rmsnorm",
    )(x, g.reshape(1, d).astype(F32))


def _mm_kernel(a_ref, w_ref, o_ref):
    o_ref[...] = jnp.dot(a_ref[...], w_ref[...], preferred_element_type=F32).astype(o_ref.dtype)


def _matmul(a, w, n_out, out_dtype, tm, tn, col0=0, name="matmul"):
    m, k = a.shape
    tm, tn = min(tm, m), min(tn, n_out)
    c0 = col0 // tn
    return pl.pallas_call(
        _mm_kernel,
        grid=(m // tm, n_out // tn),
        in_specs=[pl.BlockSpec((tm, k), lambda i, j: (i, 0)),
                  pl.BlockSpec((k, tn), lambda i, j: (0, j + c0))],
        out_specs=pl.BlockSpec((tm, tn), lambda i, j: (i, j)),
        out_shape=jax.ShapeDtypeStruct((m, n_out), out_dtype),
        compiler_params=_cparams(("parallel", "parallel")),
        name=name,
    )(a, w)


def _mm_wcast_kernel(a_ref, w_ref, *rest, has_res):
    o_ref, wbf_ref = rest[-2], rest[-1]

    @pl.when(pl.program_id(1) == 0)
    def _():
        wbf_ref[...] = w_ref[...].astype(BF16)

    acc = jnp.dot(a_ref[...], wbf_ref[...], preferred_element_type=F32)
    if has_res:
        acc = rest[0][...] + acc
    o_ref[...] = acc.astype(o_ref.dtype)


def _matmul_wcast(a, w, n_out, out_dtype, tm, tn, residual=None, name="matmul_wcast"):
    m, k = a.shape
    tm, tn = min(tm, m), min(tn, n_out)
    in_specs = [pl.BlockSpec((tm, k), lambda j, i: (i, 0)),
                pl.BlockSpec((k, tn), lambda j, i: (0, j))]
    args = [a, w]
    if residual is not None:
        in_specs.append(pl.BlockSpec((tm, tn), lambda j, i: (i, j)))
        args.append(residual)
    return pl.pallas_call(
        functools.partial(_mm_wcast_kernel, has_res=residual is not None),
        grid=(n_out // tn, m // tm),
        in_specs=in_specs,
        out_specs=pl.BlockSpec((tm, tn), lambda j, i: (i, j)),
        out_shape=jax.ShapeDtypeStruct((m, n_out), out_dtype),
        scratch_shapes=[pltpu.VMEM((k, tn), BF16)],
        compiler_params=_cparams(("parallel", "arbitrary")),
        name=name,
    )(*args)


def _sigmoid(x):
    return 0.5 * jnp.tanh(0.5 * x) + 0.5


def _branches_merge_kernel(ha_ref, wa_ref, att_ref, wb_ref, ga_ref, gb_ref, o_ref):
    ya = jnp.dot(ha_ref[...], wa_ref[...], preferred_element_type=F32)
    yb = jnp.dot(att_ref[...], wb_ref[...], preferred_element_type=F32)
    merged = _sigmoid(ga_ref[...].astype(F32)) * ya + _sigmoid(gb_ref[...].astype(F32)) * yb
    o_ref[...] = merged.astype(o_ref.dtype)


def _branches_merge(h_a, w_a, att, w_b, gates, tm, tn):
    m, k = h_a.shape
    n = w_a.shape[1]
    tm, tn = min(tm, m), min(tn, n)
    gb_blk = n // tn
    return pl.pallas_call(
        _branches_merge_kernel,
        grid=(m // tm, n // tn),
        in_specs=[pl.BlockSpec((tm, k), lambda i, j: (i, 0)),
                  pl.BlockSpec((k, tn), lambda i, j: (0, j)),
                  pl.BlockSpec((tm, k), lambda i, j: (i, 0)),
                  pl.BlockSpec((k, tn), lambda i, j: (0, j)),
                  pl.BlockSpec((tm, tn), lambda i, j: (i, j)),
                  pl.BlockSpec((tm, tn), lambda i, j: (i, j + gb_blk))],
        out_specs=pl.BlockSpec((tm, tn), lambda i, j: (i, j)),
        out_shape=jax.ShapeDtypeStruct((m, n), BF16),
        compiler_params=_cparams(("parallel", "parallel")),
        name="branches_merge",
    )(h_a, w_a, att, w_b, gates, gates)


def _rope_table_kernel(pos_ref, inv_ref, o_ref):
    ang = pos_ref[...].astype(F32) * inv_ref[...]
    lane = lax.broadcasted_iota(jnp.int32, ang.shape, 1)
    o_ref[...] = jnp.where(lane < B_ROPE, jnp.cos(ang), jnp.sin(ang))


def _rope_table(positions, tm=1024):
    n = positions.size
    inv = ROPE_BASE ** (-jnp.arange(0, B_ROPE, 2, dtype=F32) / B_ROPE)
    inv4 = jnp.tile(inv, 4).reshape(1, LANES)
    return pl.pallas_call(
        _rope_table_kernel,
        grid=(n // tm,),
        in_specs=[pl.BlockSpec((tm, 1), lambda i: (i, 0)),
                  pl.BlockSpec((1, LANES), lambda i: (0, 0))],
        out_specs=pl.BlockSpec((tm, LANES), lambda i: (i, 0)),
        out_shape=jax.ShapeDtypeStruct((n, LANES), F32),
        compiler_params=_cparams(("parallel",)),
        name="rope_table",
    )(positions.reshape(n, 1), inv4)


def _log_sigmoid(x):
    return jnp.minimum(x, 0.0) - jnp.log1p(jnp.exp(-jnp.abs(x)))


def _mlstm_kernel(zqk_ref, zv_ref, zo_ref, g_ref, cw_ref, cb_ref, gbias_ref, nw_ref, o_ref,
                  ct_ref, n_ref, m_ref, prev_ref):
    H, dk, dv, L = A_HEADS, A_QK, A_V, MLSTM_CHUNK
    c = pl.program_id(1)

    @pl.when(c == 0)
    def _():
        ct_ref[...] = jnp.zeros_like(ct_ref)
        n_ref[...] = jnp.zeros_like(n_ref)
        m_ref[...] = jnp.zeros_like(m_ref)
        prev_ref[...] = jnp.zeros_like(prev_ref)

    u = zqk_ref[...]
    prev = prev_ref[...]
    kc = cw_ref.shape[0]
    row = lax.broadcasted_iota(jnp.int32, u.shape, 0)
    y = u * cw_ref[kc - 1:kc, :] + cb_ref[...]
    for d in range(1, kc):
        shifted = jnp.where(row < d, pltpu.roll(prev, d, axis=0), pltpu.roll(u, d, axis=0))
        y = y + shifted * cw_ref[kc - 1 - d:kc - d, :]
    prev_ref[...] = u
    qk = y * jax.nn.sigmoid(y)

    gb = g_ref[...] + gbias_ref[...]
    gt = gb.T
    lsg = _log_sigmoid(gb)
    lsgt = _log_sigmoid(gt)

    ri = lax.broadcasted_iota(jnp.int32, (L, L), 0)
    ci = lax.broadcasted_iota(jnp.int32, (L, L), 1)
    tril = ci <= ri

    gate = []
    for h in range(H):
        ig_c = gb[:, h:h + 1]
        lf_c = lsg[:, H + h:H + h + 1]
        ig_r = gt[h:h + 1, :]
        lf_r = lsgt[H + h:H + h + 1, :]
        b_c = jnp.sum(jnp.where(tril, lf_r, 0.0), axis=1, keepdims=True)
        b_r = jnp.sum(jnp.where(ri <= ci, lf_c, 0.0), axis=0, keepdims=True)
        b_last = jnp.sum(lf_r, axis=1, keepdims=True)
        a_r = b_last - b_r + ig_r
        a_c = b_last - b_c + ig_c
        m_loc = jnp.max(a_r, axis=1, keepdims=True)
        wa_c = jnp.exp(a_c - m_loc)
        m_st = m_ref[h]
        dm = jnp.where(tril, b_c - b_r + ig_r, -jnp.inf)
        m_inter = b_c + m_st
        m_j = jnp.maximum(m_inter, jnp.max(dm, axis=1, keepdims=True))
        m_new = jnp.maximum(b_last + m_st, m_loc)
        gate.append(dict(wa_c=wa_c, decay=jnp.exp(dm - m_j), s_inter=jnp.exp(m_inter - m_j),
                         floor=jnp.exp(-m_j), m_new=m_new,
                         s_old=jnp.exp(b_last + m_st - m_new), s_loc=jnp.exp(m_loc - m_new)))

    state = []
    for h in range(H):
        t = gate[h]
        q = qk[:, h * dk:(h + 1) * dk]
        k = qk[:, (H + h) * dk:(H + h + 1) * dk] * (dk ** -0.5)
        v = zv_ref[:, h * dv:(h + 1) * dv]
        q_b = q.astype(BF16)
        k_b = k.astype(BF16)
        ct = ct_ref[h]
        nvec = n_ref[h]

        s = lax.dot_general(q_b, k_b, (((1,), (1,)), ((), ())), preferred_element_type=F32)
        qkw = s * t["decay"]
        num = (jnp.dot(qkw.astype(BF16), v.astype(BF16), preferred_element_type=F32)
               + t["s_inter"] * jnp.dot(q_b, ct.astype(BF16), preferred_element_type=F32))
        den = (jnp.sum(qkw, axis=1, keepdims=True)
               + t["s_inter"] * jnp.sum(q * nvec, axis=1, keepdims=True))
        hh = num / jnp.maximum(jnp.abs(den), t["floor"])

        r = lax.rsqrt(jnp.mean(hh * hh, axis=-1, keepdims=True) + EPS)
        ha = hh * r * nw_ref[h]
        og = jax.nn.sigmoid(zo_ref[:, h * dv:(h + 1) * dv])
        o_ref[:, h * dv:(h + 1) * dv] = (og * ha).astype(o_ref.dtype)

        vw = (v * t["wa_c"]).astype(BF16)
        ct_loc = jnp.dot(k.T.astype(BF16), vw, preferred_element_type=F32)
        n_loc = jnp.sum(k * t["wa_c"], axis=0, keepdims=True)
        state.append((t["s_old"] * ct + t["s_loc"] * ct_loc, t["s_old"] * nvec + t["s_loc"] * n_loc))

    for h in range(H):
        ct_ref[h], n_ref[h] = state[h]
        m_ref[h] = gate[h]["m_new"]


def _mlstm(z_main, z_tail, gate_blk, conv_w, conv_b, gbias, norm_w, batch, seq):
    H, dk, dv, L = A_HEADS, A_QK, A_V, MLSTM_CHUNK
    n = batch * seq
    nc = seq // L
    qk_w, v_w = 2 * H * dk, H * dv
    assert qk_w == v_w
    row = lambda b, c: b * nc + c
    return pl.pallas_call(
        _mlstm_kernel,
        grid=(batch, nc),
        in_specs=[pl.BlockSpec((L, qk_w), lambda b, c: (row(b, c), 0)),
                  pl.BlockSpec((L, v_w), lambda b, c: (row(b, c), 1)),
                  pl.BlockSpec((L, v_w), lambda b, c: (row(b, c), 2)),
                  pl.BlockSpec((L, LANES), lambda b, c: (row(b, c), gate_blk)),
                  pl.BlockSpec(conv_w.shape, lambda b, c: (0, 0)),
                  pl.BlockSpec((1, qk_w), lambda b, c: (0, 0)),
                  pl.BlockSpec((1, LANES), lambda b, c: (0, 0)),
                  pl.BlockSpec((H, 1, dv), lambda b, c: (0, 0, 0))],
        out_specs=pl.BlockSpec((L, v_w), lambda b, c: (row(b, c), 0)),
        out_shape=jax.ShapeDtypeStruct((n, v_w), BF16),
        scratch_shapes=[pltpu.VMEM((H, dk, dv), F32),
                        pltpu.VMEM((H, 1, dk), F32),
                        pltpu.VMEM((H, 1, 1), F32),
                        pltpu.VMEM((L, qk_w), F32)],
        compiler_params=_cparams(("parallel", "arbitrary")),
        name="mlstm",
    )(z_main, z_main, z_main, z_tail, conv_w, conv_b.reshape(1, qk_w), gbias, norm_w.reshape(H, 1, dv))


def _mla_prep_kernel(cq_ref, ckv_ref, kr_ref, cs_ref, qn_ref, kvn_ref, cqn_ref, ckvn_ref, kro_ref):
    cq = cq_ref[...]
    r = lax.rsqrt(jnp.mean(cq * cq, axis=-1, keepdims=True) + EPS)
    cqn_ref[...] = (cq * r * qn_ref[...]).astype(cqn_ref.dtype)
    ckv = ckv_ref[...]
    r = lax.rsqrt(jnp.mean(ckv * ckv, axis=-1, keepdims=True) + EPS)
    ckvn_ref[...] = (ckv * r * kvn_ref[...]).astype(ckvn_ref.dtype)
    t = kr_ref[...] * cs_ref[...]
    kro_ref[...] = t + pltpu.roll(t, B_ROPE, axis=1)


def _mla_prep(z_tail, cs, q_norm, kv_norm, cq_off, ckv_off, kr_off, tm=512):
    n = z_tail.shape[0]
    return pl.pallas_call(
        _mla_prep_kernel,
        grid=(n // tm,),
        in_specs=[pl.BlockSpec((tm, Q_LORA), lambda i: (i, cq_off // Q_LORA)),
                  pl.BlockSpec((tm, KV_LORA), lambda i: (i, ckv_off // KV_LORA)),
                  pl.BlockSpec((tm, LANES), lambda i: (i, kr_off // LANES)),
                  pl.BlockSpec((tm, LANES), lambda i: (i, 0)),
                  pl.BlockSpec((1, Q_LORA), lambda i: (0, 0)),
                  pl.BlockSpec((1, KV_LORA), lambda i: (0, 0))],
        out_specs=[pl.BlockSpec((tm, Q_LORA), lambda i: (i, 0)),
                   pl.BlockSpec((tm, KV_LORA), lambda i: (i, 0)),
                   pl.BlockSpec((tm, LANES), lambda i: (i, 0))],
        out_shape=[jax.ShapeDtypeStruct((n, Q_LORA), BF16),
                   jax.ShapeDtypeStruct((n, KV_LORA), BF16),
                   jax.ShapeDtypeStruct((n, LANES), F32)],
        compiler_params=_cparams(("parallel",)),
        name="mla_prep",
    )(z_tail, z_tail, z_tail, cs, q_norm.reshape(1, Q_LORA), kv_norm.reshape(1, KV_LORA))


def _uq_kernel(a_ref, w_ref, cs_ref, o_ref):
    scale = (B_NOPE + B_ROPE) ** -0.5 * LOG2_E
    a = a_ref[...]
    cs = cs_ref[...]
    for h in range(w_ref.shape[0]):
        acc = jnp.dot(a, w_ref[h], preferred_element_type=F32)
        t = acc[:, B_NOPE:] * cs
        rot = t + pltpu.roll(t, B_ROPE, axis=1)
        o_ref[h, :, :B_NOPE] = (acc[:, :B_NOPE] * scale).astype(o_ref.dtype)
        o_ref[h, :, B_NOPE:] = (rot[:, :B_ROPE] * scale).astype(o_ref.dtype)


def _uq_proj(cqn, w_q, cs, tm=1024, hb=4):
    n = cqn.shape[0]
    dq = B_NOPE + B_ROPE
    return pl.pallas_call(
        _uq_kernel,
        grid=(n // tm, B_HEADS // hb),
        in_specs=[pl.BlockSpec((tm, Q_LORA), lambda i, h: (i, 0)),
                  pl.BlockSpec((hb, Q_LORA, 2 * LANES), lambda i, h: (h, 0, 0)),
                  pl.BlockSpec((tm, LANES), lambda i, h: (i, 0))],
        out_specs=pl.BlockSpec((hb, tm, dq), lambda i, h: (h, i, 0)),
        out_shape=jax.ShapeDtypeStruct((B_HEADS, n, dq), BF16),
        compiler_params=_cparams(("parallel", "parallel")),
        name="uq_proj",
    )(cqn, w_q, cs)


def _ukv_kernel(a_ref, w_ref, kr_ref, k_ref, v_ref):
    a = a_ref[...]
    kr = kr_ref[:, :B_ROPE].astype(k_ref.dtype)
    for h in range(w_ref.shape[0]):
        acc = jnp.dot(a, w_ref[h], preferred_element_type=F32)
        k_ref[h, :, :B_NOPE] = acc[:, :B_NOPE].astype(k_ref.dtype)
        k_ref[h, :, B_NOPE:] = kr
        v_ref[h] = acc[:, B_NOPE:].astype(v_ref.dtype)


def _ukv_proj(ckvn, w_kv, kr, tm=1024, hb=4):
    n = ckvn.shape[0]
    dq = B_NOPE + B_ROPE
    return pl.pallas_call(
        _ukv_kernel,
        grid=(n // tm, B_HEADS // hb),
        in_specs=[pl.BlockSpec((tm, KV_LORA), lambda i, h: (i, 0)),
                  pl.BlockSpec((hb, KV_LORA, B_NOPE + B_V), lambda i, h: (h, 0, 0)),
                  pl.BlockSpec((tm, LANES), lambda i, h: (i, 0))],
        out_specs=[pl.BlockSpec((hb, tm, dq), lambda i, h: (h, i, 0)),
                   pl.BlockSpec((hb, tm, B_V), lambda i, h: (h, i, 0))],
        out_shape=[jax.ShapeDtypeStruct((B_HEADS, n, dq), BF16),
                   jax.ShapeDtypeStruct((B_HEADS, n, B_V), BF16)],
        compiler_params=_cparams(("parallel", "parallel")),
        name="ukv_proj",
    )(ckvn, w_kv, kr)


def _attn_kernel(q_ref, k_ref, v_ref, o_ref, m_ref, l_ref, acc_ref, *, tq, tk, hb):
    i = pl.program_id(2)
    dv = v_ref.shape[-1]
    ncol = tk // LANES

    m_ref[...] = jnp.full(m_ref.shape, -jnp.inf, F32)
    l_ref[...] = jnp.zeros(l_ref.shape, F32)
    acc_ref[...] = jnp.zeros(acc_ref.shape, F32)

    def step(h, row0, off, masked):
        rows = pl.ds(row0, tq - row0)
        kb = k_ref[h, pl.ds(off, tk), :]
        vb = v_ref[h, pl.ds(off, tk), :]
        s = lax.dot_general(q_ref[h, rows, :], kb, (((1,), (1,)), ((), ())), preferred_element_type=F32)
        if masked:
            qc = lax.broadcasted_iota(jnp.int32, s.shape, 0) // CHUNK
            kc = lax.broadcasted_iota(jnp.int32, s.shape, 1) // CHUNK
            s = jnp.where(kc <= qc, s, -jnp.inf)
        m_prev = m_ref[h, rows, :]
        m_new = jnp.maximum(m_prev, jnp.max(s, axis=-1, keepdims=True))
        p = jnp.exp2(s - jnp.tile(m_new, (1, ncol)))
        alpha = jnp.exp2(m_prev - m_new)
        l_ref[h, rows, :] = alpha * l_ref[h, rows, :] + jnp.sum(p, axis=-1, keepdims=True)
        acc_ref[h, rows, :] = alpha * acc_ref[h, rows, :] + jnp.dot(
            p.astype(BF16), vb, preferred_element_type=F32)
        m_ref[h, rows, :] = m_new

    def body(j, carry):
        off = pl.multiple_of(j * tk, tk)
        for h in range(hb):
            step(h, 0, off, False)
        return carry

    lax.fori_loop(0, i * (tq // tk), body, 0)
    for c in range(tq // tk):
        off = pl.multiple_of(i * tq + c * tk, tk)
        for h in range(hb):
            step(h, c * tk, off, True)
    for h in range(hb):
        o_ref[:, h * dv:(h + 1) * dv] = (acc_ref[h] / l_ref[h]).astype(o_ref.dtype)


def _attention(q, k, v, batch, seq, tq=1024, tk=512, hb=2):
    nq = seq // tq
    n = batch * seq
    dq = q.shape[-1]
    assert B_V == LANES and tk % CHUNK == 0 and tq % tk == 0
    return pl.pallas_call(
        functools.partial(_attn_kernel, tq=tq, tk=tk, hb=hb),
        grid=(batch, B_HEADS // hb, nq),
        in_specs=[pl.BlockSpec((hb, tq, dq), lambda b, h, i: (h, b * nq + i, 0)),
                  pl.BlockSpec((hb, seq, dq), lambda b, h, i: (h, b, 0)),
                  pl.BlockSpec((hb, seq, B_V), lambda b, h, i: (h, b, 0))],
        out_specs=pl.BlockSpec((tq, hb * B_V), lambda b, h, i: (b * nq + i, h)),
        out_shape=jax.ShapeDtypeStruct((n, B_HEADS * B_V), BF16),
        scratch_shapes=[pltpu.VMEM((hb, tq, LANES), F32),
                        pltpu.VMEM((hb, tq, LANES), F32),
                        pltpu.VMEM((hb, tq, B_V), F32)],
        compiler_params=_cparams(("parallel", "parallel", "arbitrary")),
        name="attention",
    )(q, k, v)


def _cast_kernel(x_ref, o_ref):
    o_ref[...] = x_ref[...].astype(o_ref.dtype)


def _cast_bf16(w, block_elems=2 * 1024 * 1024):
    cols = w.shape[-1]
    w2 = w.reshape(-1, cols)
    rows = w2.shape[0]
    bm = min(rows, 1 << ((block_elems // cols).bit_length() - 1))
    assert rows % bm == 0
    out = pl.pallas_call(
        _cast_kernel,
        grid=(rows // bm,),
        in_specs=[pl.BlockSpec((bm, cols), lambda i: (i, 0))],
        out_specs=pl.BlockSpec((bm, cols), lambda i: (i, 0)),
        out_shape=jax.ShapeDtypeStruct((rows, cols), BF16),
        compiler_params=_cparams(("parallel",)),
        name="cast_bf16",
    )(w2)
    return out.reshape(w.shape)


def _ffn_kernel(be_ref, nu_ref, x_ref, wg_ref, wu_ref, wd_ref, *rest, has_res):
    i = pl.program_id(0)
    f = pl.program_id(1)
    used = i < nu_ref[0]
    if has_res:
        res_hbm, o_ref, sem = rest
    else:
        (o_ref,) = rest

    @pl.when(f == 0)
    def _():
        if has_res:
            tm = o_ref.shape[0]
            cp = pltpu.make_async_copy(res_hbm.at[pl.ds(pl.multiple_of(i * tm, tm), tm)], o_ref, sem)
            cp.start()
            cp.wait()
        else:
            o_ref[...] = jnp.zeros_like(o_ref)

    @pl.when(used)
    def _():
        x = x_ref[...].astype(BF16)
        g = jnp.dot(x, wg_ref[0], preferred_element_type=F32)
        u = jnp.dot(x, wu_ref[0], preferred_element_type=F32)
        h1 = (g * jax.nn.sigmoid(g) * u).astype(BF16)
        o_ref[...] += jnp.dot(h1, wd_ref[0], preferred_element_type=F32)


def _ffn(x, w_gate, w_up, w_down, blk_e, n_used, residual, tm, tf, x_buffers):
    m, d = x.shape
    fdim = w_gate.shape[-1]
    nblk = m // tm

    def row_map(i, f, be, nu):
        return (jnp.minimum(i, nu[0] - 1), 0)

    in_specs = [pl.BlockSpec((tm, d), row_map, pipeline_mode=pl.Buffered(x_buffers)),
                pl.BlockSpec((1, d, tf), lambda i, f, be, nu: (be[i], 0, f)),
                pl.BlockSpec((1, d, tf), lambda i, f, be, nu: (be[i], 0, f)),
                pl.BlockSpec((1, tf, d), lambda i, f, be, nu: (be[i], f, 0))]
    args = [x, w_gate, w_up, w_down]
    scratch = []
    if residual is not None:
        assert residual.shape == (m, d) and residual.dtype == F32
        in_specs.append(pl.BlockSpec(memory_space=pl.ANY))
        args.append(residual)
        scratch.append(pltpu.SemaphoreType.DMA(()))
    return pl.pallas_call(
        functools.partial(_ffn_kernel, has_res=residual is not None),
        grid_spec=pltpu.PrefetchScalarGridSpec(
            num_scalar_prefetch=2,
            grid=(nblk, fdim // tf),
            in_specs=in_specs,
            out_specs=pl.BlockSpec((tm, d), lambda i, f, be, nu: (i, 0)),
            scratch_shapes=scratch,
        ),
        out_shape=jax.ShapeDtypeStruct((m, d), F32),
        compiler_params=_cparams(("parallel", "arbitrary")),
        name="swiglu",
    )(blk_e, n_used, *args)


def _router_kernel(x_ref, w_ref, idx_ref, gate_ref):
    x = x_ref[...]
    w = w_ref[...]
    x_hi = x.astype(BF16)
    x_lo = (x - x_hi.astype(F32)).astype(BF16)
    w_hi = w.astype(BF16)
    w_lo = (w - w_hi.astype(F32)).astype(BF16)
    logits = (jnp.dot(x_hi, w_hi, preferred_element_type=F32)
              + jnp.dot(x_hi, w_lo, preferred_element_type=F32)
              + jnp.dot(x_lo, w_hi, preferred_element_type=F32))
    lane = lax.broadcasted_iota(jnp.int32, logits.shape, 1)
    logits = jnp.where(lane < N_EXPERTS, logits, -jnp.inf)
    m1 = jnp.max(logits, axis=-1, keepdims=True)
    i1 = jnp.min(jnp.where(logits == m1, lane, LANES), axis=-1, keepdims=True)
    rest = jnp.where(lane == i1, -jnp.inf, logits)
    m2 = jnp.max(rest, axis=-1, keepdims=True)
    i2 = jnp.min(jnp.where(rest == m2, lane, LANES), axis=-1, keepdims=True)
    e2 = jnp.exp(m2 - m1)
    g1 = 1.0 / (1.0 + e2)
    g2 = e2 / (1.0 + e2)
    idx_ref[...] = jnp.where(lane == 0, i1, jnp.where(lane == 1, i2, 0))
    gate_ref[...] = jnp.where(lane == 0, g1, jnp.where(lane == 1, g2, 0.0))


def _router(h, w_router, tm=512):
    n, d = h.shape
    w_pad = jnp.zeros((d, LANES), F32).at[:, :N_EXPERTS].set(w_router)
    idx, gate = pl.pallas_call(
        _router_kernel,
        grid=(n // tm,),
        in_specs=[pl.BlockSpec((tm, d), lambda i: (i, 0)),
                  pl.BlockSpec((d, LANES), lambda i: (0, 0))],
        out_specs=[pl.BlockSpec((tm, LANES), lambda i: (i, 0)),
                   pl.BlockSpec((tm, LANES), lambda i: (i, 0))],
        out_shape=[jax.ShapeDtypeStruct((n, LANES), jnp.int32),
                   jax.ShapeDtypeStruct((n, LANES), F32)],
        compiler_params=_cparams(("parallel",)),
        name="router",
    )(h, w_pad)
    return idx[:, :TOP_K], gate


def _gather_kernel(src_ref, x_hbm, o_ref, sem, *, tr):
    base = pl.program_id(0) * tr

    def copy(r):
        return pltpu.make_async_copy(x_hbm.at[pl.ds(src_ref[base + r], 1)], o_ref.at[pl.ds(r, 1)], sem)

    def start(r, carry):
        copy(2 * r).start(priority=0)
        copy(2 * r + 1).start(priority=1)
        return carry

    lax.fori_loop(0, tr // 2, start, 0)
    pltpu.make_async_copy(x_hbm.at[pl.ds(0, tr)], o_ref, sem).wait()


def _gather_rows(x, src, tr=256):
    p = src.shape[0]
    d = x.shape[1]
    return pl.pallas_call(
        functools.partial(_gather_kernel, tr=tr),
        grid_spec=pltpu.PrefetchScalarGridSpec(
            num_scalar_prefetch=1,
            grid=(p // tr,),
            in_specs=[pl.BlockSpec(memory_space=pl.ANY)],
            out_specs=pl.BlockSpec((tr, d), lambda i, s: (i, 0)),
            scratch_shapes=[pltpu.SemaphoreType.DMA(())],
        ),
        out_shape=jax.ShapeDtypeStruct((p, d), x.dtype),
        compiler_params=_cparams(("arbitrary",)),
        name="moe_gather",
    )(src, x)


def _combine_kernel(p0_ref, p1_ref, y_hbm, x_ref, g_ref, nw_ref, o_ref, b0_ref, b1_ref, sem, *, tc, normalize):
    base = pl.program_id(0) * tc

    def copies(r):
        return (pltpu.make_async_copy(y_hbm.at[pl.ds(p0_ref[base + r], 1)], b0_ref.at[pl.ds(r, 1)], sem.at[0]),
                pltpu.make_async_copy(y_hbm.at[pl.ds(p1_ref[base + r], 1)], b1_ref.at[pl.ds(r, 1)], sem.at[1]))

    def start(r, carry):
        c0, c1 = copies(r)
        c0.start(priority=0)
        c1.start(priority=1)
        return carry

    lax.fori_loop(0, tc, start, 0)
    pltpu.make_async_copy(y_hbm.at[pl.ds(0, tc)], b0_ref, sem.at[0]).wait()
    pltpu.make_async_copy(y_hbm.at[pl.ds(0, tc)], b1_ref, sem.at[1]).wait()
    g = g_ref[...]
    x = x_ref[...] + g[:, 0:1] * b0_ref[...] + g[:, 1:2] * b1_ref[...]
    if normalize:
        r = lax.rsqrt(jnp.mean(x * x, axis=-1, keepdims=True) + EPS)
        x = x * r * nw_ref[...]
    o_ref[...] = x


def _combine(y_sorted, x, gates, pos0, pos1, norm_w, normalize, tc=256):
    n, d = x.shape
    return pl.pallas_call(
        functools.partial(_combine_kernel, tc=tc, normalize=normalize),
        grid_spec=pltpu.PrefetchScalarGridSpec(
            num_scalar_prefetch=2,
            grid=(n // tc,),
            in_specs=[pl.BlockSpec(memory_space=pl.ANY),
                      pl.BlockSpec((tc, d), lambda i, a, b: (i, 0)),
                      pl.BlockSpec((tc, LANES), lambda i, a, b: (i, 0)),
                      pl.BlockSpec((1, d), lambda i, a, b: (0, 0))],
            out_specs=pl.BlockSpec((tc, d), lambda i, a, b: (i, 0)),
            scratch_shapes=[pltpu.VMEM((tc, d), F32), pltpu.VMEM((tc, d), F32),
                            pltpu.SemaphoreType.DMA((2,))],
        ),
        out_shape=jax.ShapeDtypeStruct((n, d), F32),
        compiler_params=_cparams(("arbitrary",)),
        name="moe_combine",
    )(pos0, pos1, y_sorted, x, gates, norm_w.reshape(1, d))


def _moe_plan(idx, tm):
    n = idx.shape[0]
    a = n * TOP_K
    e_flat = idx.reshape(a)
    onehot = (e_flat[:, None] == jnp.arange(N_EXPERTS, dtype=jnp.int32)[None, :]).astype(jnp.int32)
    csum = jnp.cumsum(onehot, axis=0)
    rank = jnp.sum(onehot * (csum - 1), axis=1)
    counts = csum[-1]
    padded = (counts + tm - 1) // tm * tm
    pend = jnp.cumsum(padded)
    pstart = pend - padded
    dest = (pstart[e_flat] + rank).astype(jnp.int32)
    p = a + N_EXPERTS * tm
    nblk = p // tm
    src = jnp.zeros((p,), jnp.int32).at[dest].set(jnp.arange(a, dtype=jnp.int32) // TOP_K)
    blk_e = jnp.minimum(jnp.searchsorted(pend, jnp.arange(nblk, dtype=jnp.int32) * tm, side="right"),
                        N_EXPERTS - 1).astype(jnp.int32)
    n_used = (pend[-1:] // tm).astype(jnp.int32)
    pos = dest.reshape(n, TOP_K)
    return src, blk_e, n_used, pos[:, 0], pos[:, 1]


def _in_proj_tail(w_in_l):
    o = 2 * A_HEADS * A_QK + 2 * A_HEADS * A_V
    a_i = w_in_l[:, o:o + A_HEADS]
    a_f = w_in_l[:, o + A_HEADS:o + 2 * A_HEADS]
    o += 2 * A_HEADS
    c_q = w_in_l[:, o:o + Q_LORA]
    o += Q_LORA
    c_kv = w_in_l[:, o:o + KV_LORA]
    o += KV_LORA
    k_r = w_in_l[:, o:o + B_ROPE]
    o += B_ROPE
    d = w_in_l.shape[0]
    g_a = w_in_l[:, o:o + d]
    g_b = w_in_l[:, o + d:o + 2 * d]
    half = B_ROPE // 2
    k_r_sw = jnp.concatenate([-k_r[:, half:], k_r[:, :half]], axis=1)
    used = 2 * d + Q_LORA + KV_LORA + 2 * B_ROPE + 2 * A_HEADS
    pad = jnp.zeros((d, -used % 512), w_in_l.dtype)
    tail = jnp.concatenate([g_a, g_b, c_q, c_kv, k_r, k_r_sw, a_i, a_f, pad], axis=1)
    offs = dict(c_q=0, c_kv=Q_LORA, k_r=Q_LORA + KV_LORA, gates=Q_LORA + KV_LORA + LANES)
    return tail.astype(BF16), offs


def _uq_weight(w_uq_l):
    w = w_uq_l.reshape(Q_LORA, B_HEADS, B_NOPE + B_ROPE)
    nope, rope = w[..., :B_NOPE], w[..., B_NOPE:]
    half = B_ROPE // 2
    rope_sw = jnp.concatenate([-rope[..., half:], rope[..., :half]], axis=-1)
    return jnp.transpose(jnp.concatenate([nope, rope, rope_sw], axis=-1), (1, 0, 2)).astype(BF16)


def _ukv_weight(w_ukv_l):
    w = w_ukv_l.reshape(KV_LORA, B_HEADS, B_NOPE + B_V)
    return jnp.transpose(w, (1, 0, 2)).astype(BF16)


def _mixer(x, cs, batch, seq, norm_w, w_in_l, b_i, b_f, conv_w, conv_b, mlstm_norm, w_a,
           q_norm, kv_norm, w_uq_l, w_ukv_l, w_b, w_out_l):
    d = x.shape[1]
    h = _rmsnorm(x, norm_w, BF16)
    main_w = 2 * A_HEADS * A_QK + 2 * A_HEADS * A_V
    z_main = _matmul_wcast(h, w_in_l, main_w, F32, tm=1024, tn=512, name="in_proj_main")
    w_tail, offs = _in_proj_tail(w_in_l)
    gates = _matmul(h, w_tail, 2 * d, BF16, tm=1024, tn=512, name="in_proj_gates")
    z_rest = _matmul(h, w_tail, w_tail.shape[1] - 2 * d, F32, tm=1024, tn=512, col0=2 * d, name="in_proj_rest")

    gbias = jnp.zeros((1, LANES), F32).at[0, :A_HEADS].set(b_i).at[0, A_HEADS:2 * A_HEADS].set(b_f)
    h_a = _mlstm(z_main, z_rest, offs["gates"] // LANES, conv_w, conv_b, gbias, mlstm_norm, batch, seq)

    cqn, ckvn, kr = _mla_prep(z_rest, cs, q_norm, kv_norm, offs["c_q"], offs["c_kv"], offs["k_r"])
    q = _uq_proj(cqn, _uq_weight(w_uq_l), cs)
    k, v = _ukv_proj(ckvn, _ukv_weight(w_ukv_l), kr)
    att = _attention(q, k, v, batch, seq)
    merged = _branches_merge(h_a, w_a.astype(BF16), att, w_b.astype(BF16), gates, tm=1024, tn=512)
    return _matmul_wcast(merged, w_out_l, d, F32, tm=1024, tn=512, residual=x, name="out_proj")


def kernel(x, positions, norm_mix, w_in, b_igate, b_fgate, conv_w, conv_b, mlstm_norm, w_branch_a, q_norm, kv_norm, w_uq, w_ukv, w_branch_b, w_out, norm_ffn, w_gate_d, w_up_d, w_down_d, w_router, w_gate_e, w_up_e, w_down_e, norm_final):
    batch, seq, d = x.shape
    n = batch * seq
    depth = norm_mix.shape[0]
    x = x.reshape(n, d)
    cs = _rope_table(positions)
    wg_d, wu_d, wd_d = _cast_bf16(w_gate_d), _cast_bf16(w_up_d), _cast_bf16(w_down_d)
    wg_e, wu_e, wd_e = _cast_bf16(w_gate_e), _cast_bf16(w_up_e), _cast_bf16(w_down_e)
    out = None
    for l in range(depth):
        x = _mixer(x, cs, batch, seq, norm_mix[l], w_in[l], b_igate[l], b_fgate[l], conv_w[l], conv_b[l],
                   mlstm_norm[l], w_branch_a[l], q_norm[l], kv_norm[l], w_uq[l], w_ukv[l],
                   w_branch_b[l], w_out[l])
        last = l == depth - 1
        if l % 2 == 0:
            h = _rmsnorm(x, norm_ffn[l], BF16)
            tm = 512
            blk_e = jnp.zeros((n // tm,), jnp.int32)
            n_used = jnp.full((1,), n // tm, jnp.int32)
            e = l // 2
            x = _ffn(h, wg_d[e:e + 1], wu_d[e:e + 1], wd_d[e:e + 1], blk_e, n_used, x,
                     tm=tm, tf=512, x_buffers=2)
            if last:
                out = _rmsnorm(x, norm_final, F32)
        else:
            tm = 512
            h = _rmsnorm(x, norm_ffn[l], F32)
            idx, gates = _router(h, w_router[l // 2])
            src, blk_e, n_used, pos0, pos1 = _moe_plan(idx, tm)
            xs = _gather_rows(h, src)
            ys = _ffn(xs, wg_e[l // 2], wu_e[l // 2], wd_e[l // 2], blk_e, n_used, None,
                      tm=tm, tf=512, x_buffers=1)
            x = _combine(ys, x, gates, pos0, pos1, norm_final, normalize=last)
            if last:
                out = x
    return out.reshape(batch, seq, d)
```

```python
import functools

import jax
import jax.numpy as jnp
from jax import lax
from jax.experimental import pallas as pl
from jax.experimental.pallas import tpu as pltpu

F32 = jnp.float32
BF16 = jnp.bfloat16

EPS = 1e-6
CHUNK = 64
MLSTM_CHUNK = 128
A_HEADS = 8
A_QK = 128
A_V = 256
B_HEADS = 16
B_NOPE = 128
B_ROPE = 64
B_V = 128
Q_LORA = 1024
KV_LORA = 512
ROPE_BASE = 10000.0
N_EXPERTS = 8
TOP_K = 2
LOG2_E = 1.4426950408889634

LANES = 128
VMEM_LIMIT = 56 * 1024 * 1024


def _cparams(sem):
    return pltpu.CompilerParams(dimension_semantics=sem, vmem_limit_bytes=VMEM_LIMIT)


def _rmsnorm_kernel(x_ref, g_ref, o_ref):
    x = x_ref[...].astype(F32)
    r = lax.rsqrt(jnp.mean(x * x, axis=-1, keepdims=True) + EPS)
    o_ref[...] = (x * r * g_ref[...]).astype(o_ref.dtype)


def _rmsnorm(x, g, out_dtype, tm=512):
    m, d = x.shape
    return pl.pallas_call(
        _rmsnorm_kernel,
        grid=(m // tm,),
        in_specs=[pl.BlockSpec((tm, d), lambda i: (i, 0)),
                  pl.BlockSpec((1, d), lambda i: (0, 0))],
        out_specs=pl.BlockSpec((tm, d), lambda i: (i, 0)),
        out_shape=jax.ShapeDtypeStruct((m, d), out_dtype),
        compiler_params=_cparams(("parallel",)),
        name="rmsnorm",
    )(x, g.reshape(1, d).astype(F32))


def _mm_kernel(a_ref, w_ref, o_ref):
    o_ref[...] = jnp.dot(a_ref[...], w_ref[...], preferred_element_type=F32).astype(o_ref.dtype)


def _matmul(a, w, n_out, out_dtype, tm, tn, col0=0, name="matmul"):
    m, k = a.shape
    tm, tn = min(tm, m), min(tn, n_out)
    c0 = col0 // tn
    return pl.pallas_call(
        _mm_kernel,
        grid=(m // tm, n_out // tn),
        in_specs=[pl.BlockSpec((tm, k), lambda i, j: (i, 0)),
                  pl.BlockSpec((k, tn), lambda i, j: (0, j + c0))],
        out_specs=pl.BlockSpec((tm, tn), lambda i, j: (i, j)),
        out_shape=jax.ShapeDtypeStruct((m, n_out), out_dtype),
        compiler_params=_cparams(("parallel", "parallel")),
        name=name,
    )(a, w)


def _mm_wcast_kernel(a_ref, w_ref, *rest, has_res):
    o_ref, wbf_ref = rest[-2], rest[-1]

    @pl.when(pl.program_id(1) == 0)
    def _():
        wbf_ref[...] = w_ref[0].astype(BF16)

    acc = jnp.dot(a_ref[...], wbf_ref[...], preferred_element_type=F32)
    if has_res:
        acc = rest[0][...] + acc
    o_ref[...] = acc.astype(o_ref.dtype)


def _matmul_wcast(a, w_stack, layer, n_out, out_dtype, tm, tn, residual=None, name="matmul_wcast"):
    m, k = a.shape
    tm, tn = min(tm, m), min(tn, n_out)
    in_specs = [pl.BlockSpec((tm, k), lambda j, i: (i, 0)),
                pl.BlockSpec((1, k, tn), lambda j, i: (layer, 0, j))]
    args = [a, w_stack]
    if residual is not None:
        in_specs.append(pl.BlockSpec((tm, tn), lambda j, i: (i, j)))
        args.append(residual)
    return pl.pallas_call(
        functools.partial(_mm_wcast_kernel, has_res=residual is not None),
        grid=(n_out // tn, m // tm),
        in_specs=in_specs,
        out_specs=pl.BlockSpec((tm, tn), lambda j, i: (i, j)),
        out_shape=jax.ShapeDtypeStruct((m, n_out), out_dtype),
        scratch_shapes=[pltpu.VMEM((k, tn), BF16)],
        compiler_params=_cparams(("parallel", "arbitrary")),
        name=name,
    )(*args)


def _sigmoid(x):
    return 0.5 * jnp.tanh(0.5 * x) + 0.5


def _branches_merge_kernel(ha_ref, wa_ref, att_ref, wb_ref, ga_ref, gb_ref, o_ref):
    ya = jnp.dot(ha_ref[...], wa_ref[...], preferred_element_type=F32)
    yb = jnp.dot(att_ref[...], wb_ref[...], preferred_element_type=F32)
    merged = _sigmoid(ga_ref[...].astype(F32)) * ya + _sigmoid(gb_ref[...].astype(F32)) * yb
    o_ref[...] = merged.astype(o_ref.dtype)


def _branches_merge(h_a, w_a, att, w_b, gates, tm, tn):
    m, k = h_a.shape
    n = w_a.shape[1]
    tm, tn = min(tm, m), min(tn, n)
    gb_blk = n // tn
    return pl.pallas_call(
        _branches_merge_kernel,
        grid=(m // tm, n // tn),
        in_specs=[pl.BlockSpec((tm, k), lambda i, j: (i, 0)),
                  pl.BlockSpec((k, tn), lambda i, j: (0, j)),
                  pl.BlockSpec((tm, k), lambda i, j: (i, 0)),
                  pl.BlockSpec((k, tn), lambda i, j: (0, j)),
                  pl.BlockSpec((tm, tn), lambda i, j: (i, j)),
                  pl.BlockSpec((tm, tn), lambda i, j: (i, j + gb_blk))],
        out_specs=pl.BlockSpec((tm, tn), lambda i, j: (i, j)),
        out_shape=jax.ShapeDtypeStruct((m, n), BF16),
        compiler_params=_cparams(("parallel", "parallel")),
        name="branches_merge",
    )(h_a, w_a, att, w_b, gates, gates)


def _rope_table_kernel(pos_ref, inv_ref, o_ref):
    ang = pos_ref[...].astype(F32) * inv_ref[...]
    lane = lax.broadcasted_iota(jnp.int32, ang.shape, 1)
    o_ref[...] = jnp.where(lane < B_ROPE, jnp.cos(ang), jnp.sin(ang))


def _rope_table(positions, tm=1024):
    n = positions.size
    inv = ROPE_BASE ** (-jnp.arange(0, B_ROPE, 2, dtype=F32) / B_ROPE)
    inv4 = jnp.tile(inv, 4).reshape(1, LANES)
    return pl.pallas_call(
        _rope_table_kernel,
        grid=(n // tm,),
        in_specs=[pl.BlockSpec((tm, 1), lambda i: (i, 0)),
                  pl.BlockSpec((1, LANES), lambda i: (0, 0))],
        out_specs=pl.BlockSpec((tm, LANES), lambda i: (i, 0)),
        out_shape=jax.ShapeDtypeStruct((n, LANES), F32),
        compiler_params=_cparams(("parallel",)),
        name="rope_table",
    )(positions.reshape(n, 1), inv4)


def _log_sigmoid(x):
    return jnp.minimum(x, 0.0) - jnp.log1p(jnp.exp(-jnp.abs(x)))


def _mlstm_kernel(zqk_ref, zv_ref, zo_ref, g_ref, cw_ref, cb_ref, gbias_ref, nw_ref, o_ref,
                  ct_ref, n_ref, m_ref, prev_ref):
    H, dk, dv, L = A_HEADS, A_QK, A_V, MLSTM_CHUNK
    c = pl.program_id(1)

    @pl.when(c == 0)
    def _():
        ct_ref[...] = jnp.zeros_like(ct_ref)
        n_ref[...] = jnp.zeros_like(n_ref)
        m_ref[...] = jnp.zeros_like(m_ref)
        prev_ref[...] = jnp.zeros_like(prev_ref)

    u = zqk_ref[...]
    prev = prev_ref[...]
    kc = cw_ref.shape[0]
    row = lax.broadcasted_iota(jnp.int32, u.shape, 0)
    y = u * cw_ref[kc - 1:kc, :] + cb_ref[...]
    for d in range(1, kc):
        shifted = jnp.where(row < d, pltpu.roll(prev, d, axis=0), pltpu.roll(u, d, axis=0))
        y = y + shifted * cw_ref[kc - 1 - d:kc - d, :]
    prev_ref[...] = u
    qk = y * jax.nn.sigmoid(y)

    gb = g_ref[...] + gbias_ref[...]
    gt = gb.T
    lsg = _log_sigmoid(gb)
    lsgt = _log_sigmoid(gt)

    ri = lax.broadcasted_iota(jnp.int32, (L, L), 0)
    ci = lax.broadcasted_iota(jnp.int32, (L, L), 1)
    tril = ci <= ri

    gate = []
    for h in range(H):
        ig_c = gb[:, h:h + 1]
        lf_c = lsg[:, H + h:H + h + 1]
        ig_r = gt[h:h + 1, :]
        lf_r = lsgt[H + h:H + h + 1, :]
        b_c = jnp.sum(jnp.where(tril, lf_r, 0.0), axis=1, keepdims=True)
        b_r = jnp.sum(jnp.where(ri <= ci, lf_c, 0.0), axis=0, keepdims=True)
        b_last = jnp.sum(lf_r, axis=1, keepdims=True)
        a_r = b_last - b_r + ig_r
        a_c = b_last - b_c + ig_c
        m_loc = jnp.max(a_r, axis=1, keepdims=True)
        wa_c = jnp.exp(a_c - m_loc)
        m_st = m_ref[h]
        dm = jnp.where(tril, b_c - b_r + ig_r, -jnp.inf)
        m_inter = b_c + m_st
        m_j = jnp.maximum(m_inter, jnp.max(dm, axis=1, keepdims=True))
        m_new = jnp.maximum(b_last + m_st, m_loc)
        gate.append(dict(wa_c=wa_c, decay=jnp.exp(dm - m_j), s_inter=jnp.exp(m_inter - m_j),
                         floor=jnp.exp(-m_j), m_new=m_new,
                         s_old=jnp.exp(b_last + m_st - m_new), s_loc=jnp.exp(m_loc - m_new)))

    state = []
    for h in range(H):
        t = gate[h]
        q = qk[:, h * dk:(h + 1) * dk]
        k = qk[:, (H + h) * dk:(H + h + 1) * dk] * (dk ** -0.5)
        v = zv_ref[:, h * dv:(h + 1) * dv]
        q_b = q.astype(BF16)
        k_b = k.astype(BF16)
        ct = ct_ref[h]
        nvec = n_ref[h]

        s = lax.dot_general(q_b, k_b, (((1,), (1,)), ((), ())), preferred_element_type=F32)
        qkw = s * t["decay"]
        num = (jnp.dot(qkw.astype(BF16), v.astype(BF16), preferred_element_type=F32)
               + t["s_inter"] * jnp.dot(q_b, ct.astype(BF16), preferred_element_type=F32))
        den = (jnp.sum(qkw, axis=1, keepdims=True)
               + t["s_inter"] * jnp.sum(q * nvec, axis=1, keepdims=True))
        hh = num / jnp.maximum(jnp.abs(den), t["floor"])

        r = lax.rsqrt(jnp.mean(hh * hh, axis=-1, keepdims=True) + EPS)
        ha = hh * r * nw_ref[h]
        og = jax.nn.sigmoid(zo_ref[:, h * dv:(h + 1) * dv])
        o_ref[:, h * dv:(h + 1) * dv] = (og * ha).astype(o_ref.dtype)

        vw = (v * t["wa_c"]).astype(BF16)
        ct_loc = jnp.dot(k.T.astype(BF16), vw, preferred_element_type=F32)
        n_loc = jnp.sum(k * t["wa_c"], axis=0, keepdims=True)
        state.append((t["s_old"] * ct + t["s_loc"] * ct_loc, t["s_old"] * nvec + t["s_loc"] * n_loc))

    for h in range(H):
        ct_ref[h], n_ref[h] = state[h]
        m_ref[h] = gate[h]["m_new"]


def _mlstm(z_main, z_tail, gate_blk, conv_w, conv_b, gbias, norm_w, batch, seq):
    H, dk, dv, L = A_HEADS, A_QK, A_V, MLSTM_CHUNK
    n = batch * seq
    nc = seq // L
    qk_w, v_w = 2 * H * dk, H * dv
    assert qk_w == v_w
    row = lambda b, c: b * nc + c
    return pl.pallas_call(
        _mlstm_kernel,
        grid=(batch, nc),
        in_specs=[pl.BlockSpec((L, qk_w), lambda b, c: (row(b, c), 0)),
                  pl.BlockSpec((L, v_w), lambda b, c: (row(b, c), 1)),
                  pl.BlockSpec((L, v_w), lambda b, c: (row(b, c), 2)),
                  pl.BlockSpec((L, LANES), lambda b, c: (row(b, c), gate_blk)),
                  pl.BlockSpec(conv_w.shape, lambda b, c: (0, 0)),
                  pl.BlockSpec((1, qk_w), lambda b, c: (0, 0)),
                  pl.BlockSpec((1, LANES), lambda b, c: (0, 0)),
                  pl.BlockSpec((H, 1, dv), lambda b, c: (0, 0, 0))],
        out_specs=pl.BlockSpec((L, v_w), lambda b, c: (row(b, c), 0)),
        out_shape=jax.ShapeDtypeStruct((n, v_w), BF16),
        scratch_shapes=[pltpu.VMEM((H, dk, dv), F32),
                        pltpu.VMEM((H, 1, dk), F32),
                        pltpu.VMEM((H, 1, 1), F32),
                        pltpu.VMEM((L, qk_w), F32)],
        compiler_params=_cparams(("parallel", "arbitrary")),
        name="mlstm",
    )(z_main, z_main, z_main, z_tail, conv_w, conv_b.reshape(1, qk_w), gbias, norm_w.reshape(H, 1, dv))


def _mla_prep_kernel(cq_ref, ckv_ref, kr_ref, cs_ref, qn_ref, kvn_ref, cqn_ref, ckvn_ref, kro_ref):
    cq = cq_ref[...]
    r = lax.rsqrt(jnp.mean(cq * cq, axis=-1, keepdims=True) + EPS)
    cqn_ref[...] = (cq * r * qn_ref[...]).astype(cqn_ref.dtype)
    ckv = ckv_ref[...]
    r = lax.rsqrt(jnp.mean(ckv * ckv, axis=-1, keepdims=True) + EPS)
    ckvn_ref[...] = (ckv * r * kvn_ref[...]).astype(ckvn_ref.dtype)
    t = kr_ref[...] * cs_ref[...]
    kro_ref[...] = t + pltpu.roll(t, B_ROPE, axis=1)


def _mla_prep(z_tail, cs, q_norm, kv_norm, cq_off, ckv_off, kr_off, tm=512):
    n = z_tail.shape[0]
    return pl.pallas_call(
        _mla_prep_kernel,
        grid=(n // tm,),
        in_specs=[pl.BlockSpec((tm, Q_LORA), lambda i: (i, cq_off // Q_LORA)),
                  pl.BlockSpec((tm, KV_LORA), lambda i: (i, ckv_off // KV_LORA)),
                  pl.BlockSpec((tm, LANES), lambda i: (i, kr_off // LANES)),
                  pl.BlockSpec((tm, LANES), lambda i: (i, 0)),
                  pl.BlockSpec((1, Q_LORA), lambda i: (0, 0)),
                  pl.BlockSpec((1, KV_LORA), lambda i: (0, 0))],
        out_specs=[pl.BlockSpec((tm, Q_LORA), lambda i: (i, 0)),
                   pl.BlockSpec((tm, KV_LORA), lambda i: (i, 0)),
                   pl.BlockSpec((tm, LANES), lambda i: (i, 0))],
        out_shape=[jax.ShapeDtypeStruct((n, Q_LORA), BF16),
                   jax.ShapeDtypeStruct((n, KV_LORA), BF16),
                   jax.ShapeDtypeStruct((n, LANES), F32)],
        compiler_params=_cparams(("parallel",)),
        name="mla_prep",
    )(z_tail, z_tail, z_tail, cs, q_norm.reshape(1, Q_LORA), kv_norm.reshape(1, KV_LORA))


def _uq_kernel(a_ref, w_ref, cs_ref, o_ref):
    scale = (B_NOPE + B_ROPE) ** -0.5 * LOG2_E
    a = a_ref[...]
    cs = cs_ref[...]
    for h in range(w_ref.shape[0]):
        acc = jnp.dot(a, w_ref[h], preferred_element_type=F32)
        t = acc[:, B_NOPE:] * cs
        rot = t + pltpu.roll(t, B_ROPE, axis=1)
        o_ref[h, :, :B_NOPE] = (acc[:, :B_NOPE] * scale).astype(o_ref.dtype)
        o_ref[h, :, B_NOPE:] = (rot[:, :B_ROPE] * scale).astype(o_ref.dtype)


def _uq_proj(cqn, w_q, cs, tm=1024, hb=4):
    n = cqn.shape[0]
    dq = B_NOPE + B_ROPE
    return pl.pallas_call(
        _uq_kernel,
        grid=(n // tm, B_HEADS // hb),
        in_specs=[pl.BlockSpec((tm, Q_LORA), lambda i, h: (i, 0)),
                  pl.BlockSpec((hb, Q_LORA, 2 * LANES), lambda i, h: (h, 0, 0)),
                  pl.BlockSpec((tm, LANES), lambda i, h: (i, 0))],
        out_specs=pl.BlockSpec((hb, tm, dq), lambda i, h: (h, i, 0)),
        out_shape=jax.ShapeDtypeStruct((B_HEADS, n, dq), BF16),
        compiler_params=_cparams(("parallel", "parallel")),
        name="uq_proj",
    )(cqn, w_q, cs)


def _ukv_kernel(a_ref, w_ref, kr_ref, k_ref, v_ref):
    a = a_ref[...]
    kr = kr_ref[:, :B_ROPE].astype(k_ref.dtype)
    for h in range(w_ref.shape[0]):
        acc = jnp.dot(a, w_ref[h], preferred_element_type=F32)
        k_ref[h, :, :B_NOPE] = acc[:, :B_NOPE].astype(k_ref.dtype)
        k_ref[h, :, B_NOPE:] = kr
        v_ref[h] = acc[:, B_NOPE:].astype(v_ref.dtype)


def _ukv_proj(ckvn, w_kv, kr, tm=1024, hb=4):
    n = ckvn.shape[0]
    dq = B_NOPE + B_ROPE
    return pl.pallas_call(
        _ukv_kernel,
        grid=(n // tm, B_HEADS // hb),
        in_specs=[pl.BlockSpec((tm, KV_LORA), lambda i, h: (i, 0)),
                  pl.BlockSpec((hb, KV_LORA, B_NOPE + B_V), lambda i, h: (h, 0, 0)),
                  pl.BlockSpec((tm, LANES), lambda i, h: (i, 0))],
        out_specs=[pl.BlockSpec((hb, tm, dq), lambda i, h: (h, i, 0)),
                   pl.BlockSpec((hb, tm, B_V), lambda i, h: (h, i, 0))],
        out_shape=[jax.ShapeDtypeStruct((B_HEADS, n, dq), BF16),
                   jax.ShapeDtypeStruct((B_HEADS, n, B_V), BF16)],
        compiler_params=_cparams(("parallel", "parallel")),
        name="ukv_proj",
    )(ckvn, w_kv, kr)


def _attn_kernel(q_ref, k_ref, v_ref, o_ref, m_ref, l_ref, acc_ref, *, tq, tk, hb):
    i = pl.program_id(2)
    dv = v_ref.shape[-1]
    ncol = tk // LANES

    m_ref[...] = jnp.full(m_ref.shape, -jnp.inf, F32)
    l_ref[...] = jnp.zeros(l_ref.shape, F32)
    acc_ref[...] = jnp.zeros(acc_ref.shape, F32)

    def step(h, row0, off, masked):
        rows = pl.ds(row0, tq - row0)
        kb = k_ref[h, pl.ds(off, tk), :]
        vb = v_ref[h, pl.ds(off, tk), :]
        s = lax.dot_general(q_ref[h, rows, :], kb, (((1,), (1,)), ((), ())), preferred_element_type=F32)
        if masked:
            qc = lax.broadcasted_iota(jnp.int32, s.shape, 0) // CHUNK
            kc = lax.broadcasted_iota(jnp.int32, s.shape, 1) // CHUNK
            s = jnp.where(kc <= qc, s, -jnp.inf)
        m_prev = m_ref[h, rows, :]
        m_new = jnp.maximum(m_prev, jnp.max(s, axis=-1, keepdims=True))
        p = jnp.exp2(s - jnp.tile(m_new, (1, ncol)))
        alpha = jnp.exp2(m_prev - m_new)
        l_ref[h, rows, :] = alpha * l_ref[h, rows, :] + jnp.sum(p, axis=-1, keepdims=True)
        acc_ref[h, rows, :] = alpha * acc_ref[h, rows, :] + jnp.dot(
            p.astype(BF16), vb, preferred_element_type=F32)
        m_ref[h, rows, :] = m_new

    def body(j, carry):
        off = pl.multiple_of(j * tk, tk)
        for h in range(hb):
            step(h, 0, off, False)
        return carry

    lax.fori_loop(0, i * (tq // tk), body, 0)
    for c in range(tq // tk):
        off = pl.multiple_of(i * tq + c * tk, tk)
        for h in range(hb):
            step(h, c * tk, off, True)
    for h in range(hb):
        o_ref[:, h * dv:(h + 1) * dv] = (acc_ref[h] / l_ref[h]).astype(o_ref.dtype)


def _attention(q, k, v, batch, seq, tq=1024, tk=512, hb=2):
    nq = seq // tq
    n = batch * seq
    dq = q.shape[-1]
    assert B_V == LANES and tk % CHUNK == 0 and tq % tk == 0
    return pl.pallas_call(
        functools.partial(_attn_kernel, tq=tq, tk=tk, hb=hb),
        grid=(batch, B_HEADS // hb, nq),
        in_specs=[pl.BlockSpec((hb, tq, dq), lambda b, h, i: (h, b * nq + i, 0)),
                  pl.BlockSpec((hb, seq, dq), lambda b, h, i: (h, b, 0)),
                  pl.BlockSpec((hb, seq, B_V), lambda b, h, i: (h, b, 0))],
        out_specs=pl.BlockSpec((tq, hb * B_V), lambda b, h, i: (b * nq + i, h)),
        out_shape=jax.ShapeDtypeStruct((n, B_HEADS * B_V), BF16),
        scratch_shapes=[pltpu.VMEM((hb, tq, LANES), F32),
                        pltpu.VMEM((hb, tq, LANES), F32),
                        pltpu.VMEM((hb, tq, B_V), F32)],
        compiler_params=_cparams(("parallel", "parallel", "arbitrary")),
        name="attention",
    )(q, k, v)


def _cast_kernel(x_ref, o_ref):
    o_ref[...] = x_ref[...].astype(o_ref.dtype)


def _cast_bf16(w, block_elems=2 * 1024 * 1024):
    cols = w.shape[-1]
    w2 = w.reshape(-1, cols)
    rows = w2.shape[0]
    bm = min(rows, 1 << ((block_elems // cols).bit_length() - 1))
    assert rows % bm == 0
    out = pl.pallas_call(
        _cast_kernel,
        grid=(rows // bm,),
        in_specs=[pl.BlockSpec((bm, cols), lambda i: (i, 0))],
        out_specs=pl.BlockSpec((bm, cols), lambda i: (i, 0)),
        out_shape=jax.ShapeDtypeStruct((rows, cols), BF16),
        compiler_params=_cparams(("parallel",)),
        name="cast_bf16",
    )(w2)
    return out.reshape(w.shape)


def _ffn_kernel(be_ref, nu_ref, x_ref, wg_ref, wu_ref, wd_ref, *rest, has_res):
    i = pl.program_id(0)
    f = pl.program_id(1)
    used = i < nu_ref[0]
    if has_res:
        res_hbm, o_ref, sem = rest
    else:
        (o_ref,) = rest

    @pl.when(f == 0)
    def _():
        if has_res:
            tm = o_ref.shape[0]
            cp = pltpu.make_async_copy(res_hbm.at[pl.ds(pl.multiple_of(i * tm, tm), tm)], o_ref, sem)
            cp.start()
            cp.wait()
        else:
            o_ref[...] = jnp.zeros_like(o_ref)

    @pl.when(used)
    def _():
        x = x_ref[...].astype(BF16)
        g = jnp.dot(x, wg_ref[0], preferred_element_type=F32)
        u = jnp.dot(x, wu_ref[0], preferred_element_type=F32)
        h1 = (g * jax.nn.sigmoid(g) * u).astype(BF16)
        o_ref[...] += jnp.dot(h1, wd_ref[0], preferred_element_type=F32)


def _ffn(x, w_gate, w_up, w_down, blk_e, n_used, residual, tm, tf, x_buffers):
    m, d = x.shape
    fdim = w_gate.shape[-1]
    nblk = m // tm

    def row_map(i, f, be, nu):
        return (jnp.minimum(i, nu[0] - 1), 0)

    in_specs = [pl.BlockSpec((tm, d), row_map, pipeline_mode=pl.Buffered(x_buffers)),
                pl.BlockSpec((1, d, tf), lambda i, f, be, nu: (be[i], 0, f)),
                pl.BlockSpec((1, d, tf), lambda i, f, be, nu: (be[i], 0, f)),
                pl.BlockSpec((1, tf, d), lambda i, f, be, nu: (be[i], f, 0))]
    args = [x, w_gate, w_up, w_down]
    scratch = []
    if residual is not None:
        assert residual.shape == (m, d) and residual.dtype == F32
        in_specs.append(pl.BlockSpec(memory_space=pl.ANY))
        args.append(residual)
        scratch.append(pltpu.SemaphoreType.DMA(()))
    return pl.pallas_call(
        functools.partial(_ffn_kernel, has_res=residual is not None),
        grid_spec=pltpu.PrefetchScalarGridSpec(
            num_scalar_prefetch=2,
            grid=(nblk, fdim // tf),
            in_specs=in_specs,
            out_specs=pl.BlockSpec((tm, d), lambda i, f, be, nu: (i, 0)),
            scratch_shapes=scratch,
        ),
        out_shape=jax.ShapeDtypeStruct((m, d), F32),
        compiler_params=_cparams(("parallel", "arbitrary")),
        name="swiglu",
    )(blk_e, n_used, *args)


def _router_kernel(x_ref, w_ref, idx_ref, gate_ref):
    x = x_ref[...]
    w = w_ref[...]
    x_hi = x.astype(BF16)
    x_lo = (x - x_hi.astype(F32)).astype(BF16)
    w_hi = w.astype(BF16)
    w_lo = (w - w_hi.astype(F32)).astype(BF16)
    logits = (jnp.dot(x_hi, w_hi, preferred_element_type=F32)
              + jnp.dot(x_hi, w_lo, preferred_element_type=F32)
              + jnp.dot(x_lo, w_hi, preferred_element_type=F32))
    lane = lax.broadcasted_iota(jnp.int32, logits.shape, 1)
    logits = jnp.where(lane < N_EXPERTS, logits, -jnp.inf)
    m1 = jnp.max(logits, axis=-1, keepdims=True)
    i1 = jnp.min(jnp.where(logits == m1, lane, LANES), axis=-1, keepdims=True)
    rest = jnp.where(lane == i1, -jnp.inf, logits)
    m2 = jnp.max(rest, axis=-1, keepdims=True)
    i2 = jnp.min(jnp.where(rest == m2, lane, LANES), axis=-1, keepdims=True)
    e2 = jnp.exp(m2 - m1)
    g1 = 1.0 / (1.0 + e2)
    g2 = e2 / (1.0 + e2)
    idx_ref[...] = jnp.where(lane == 0, i1, jnp.where(lane == 1, i2, 0))
    gate_ref[...] = jnp.where(lane == 0, g1, jnp.where(lane == 1, g2, 0.0))


def _router(h, w_router, tm=512):
    n, d = h.shape
    w_pad = jnp.zeros((d, LANES), F32).at[:, :N_EXPERTS].set(w_router)
    idx, gate = pl.pallas_call(
        _router_kernel,
        grid=(n // tm,),
        in_specs=[pl.BlockSpec((tm, d), lambda i: (i, 0)),
                  pl.BlockSpec((d, LANES), lambda i: (0, 0))],
        out_specs=[pl.BlockSpec((tm, LANES), lambda i: (i, 0)),
                   pl.BlockSpec((tm, LANES), lambda i: (i, 0))],
        out_shape=[jax.ShapeDtypeStruct((n, LANES), jnp.int32),
                   jax.ShapeDtypeStruct((n, LANES), F32)],
        compiler_params=_cparams(("parallel",)),
        name="router",
    )(h, w_pad)
    return idx[:, :TOP_K], gate


def _gather_kernel(src_ref, x_hbm, o_ref, sem, *, tr):
    base = pl.program_id(0) * tr

    def copy(r):
        return pltpu.make_async_copy(x_hbm.at[pl.ds(src_ref[base + r], 1)], o_ref.at[pl.ds(r, 1)], sem)

    def start(r, carry):
        copy(2 * r).start(priority=0)
        copy(2 * r + 1).start(priority=1)
        return carry

    lax.fori_loop(0, tr // 2, start, 0)
    pltpu.make_async_copy(x_hbm.at[pl.ds(0, tr)], o_ref, sem).wait()


def _gather_rows(x, src, tr=256):
    p = src.shape[0]
    d = x.shape[1]
    return pl.pallas_call(
        functools.partial(_gather_kernel, tr=tr),
        grid_spec=pltpu.PrefetchScalarGridSpec(
            num_scalar_prefetch=1,
            grid=(p // tr,),
            in_specs=[pl.BlockSpec(memory_space=pl.ANY)],
            out_specs=pl.BlockSpec((tr, d), lambda i, s: (i, 0)),
            scratch_shapes=[pltpu.SemaphoreType.DMA(())],
        ),
        out_shape=jax.ShapeDtypeStruct((p, d), x.dtype),
        compiler_params=_cparams(("arbitrary",)),
        name="moe_gather",
    )(src, x)


def _combine_kernel(p0_ref, p1_ref, y_hbm, x_ref, g_ref, nw_ref, o_ref, b0_ref, b1_ref, sem, *, tc, normalize):
    base = pl.program_id(0) * tc

    def copies(r):
        return (pltpu.make_async_copy(y_hbm.at[pl.ds(p0_ref[base + r], 1)], b0_ref.at[pl.ds(r, 1)], sem.at[0]),
                pltpu.make_async_copy(y_hbm.at[pl.ds(p1_ref[base + r], 1)], b1_ref.at[pl.ds(r, 1)], sem.at[1]))

    def start(r, carry):
        c0, c1 = copies(r)
        c0.start(priority=0)
        c1.start(priority=1)
        return carry

    lax.fori_loop(0, tc, start, 0)
    pltpu.make_async_copy(y_hbm.at[pl.ds(0, tc)], b0_ref, sem.at[0]).wait()
    pltpu.make_async_copy(y_hbm.at[pl.ds(0, tc)], b1_ref, sem.at[1]).wait()
    g = g_ref[...]
    x = x_ref[...] + g[:, 0:1] * b0_ref[...] + g[:, 1:2] * b1_ref[...]
    if normalize:
        r = lax.rsqrt(jnp.mean(x * x, axis=-1, keepdims=True) + EPS)
        x = x * r * nw_ref[...]
    o_ref[...] = x


def _combine(y_sorted, x, gates, pos0, pos1, norm_w, normalize, tc=256):
    n, d = x.shape
    return pl.pallas_call(
        functools.partial(_combine_kernel, tc=tc, normalize=normalize),
        grid_spec=pltpu.PrefetchScalarGridSpec(
            num_scalar_prefetch=2,
            grid=(n // tc,),
            in_specs=[pl.BlockSpec(memory_space=pl.ANY),
                      pl.BlockSpec((tc, d), lambda i, a, b: (i, 0)),
                      pl.BlockSpec((tc, LANES), lambda i, a, b: (i, 0)),
                      pl.BlockSpec((1, d), lambda i, a, b: (0, 0))],
            out_specs=pl.BlockSpec((tc, d), lambda i, a, b: (i, 0)),
            scratch_shapes=[pltpu.VMEM((tc, d), F32), pltpu.VMEM((tc, d), F32),
                            pltpu.SemaphoreType.DMA((2,))],
        ),
        out_shape=jax.ShapeDtypeStruct((n, d), F32),
        compiler_params=_cparams(("arbitrary",)),
        name="moe_combine",
    )(pos0, pos1, y_sorted, x, gates, norm_w.reshape(1, d))


def _moe_plan(idx, tm):
    n = idx.shape[0]
    a = n * TOP_K
    e_flat = idx.reshape(a)
    onehot = (e_flat[:, None] == jnp.arange(N_EXPERTS, dtype=jnp.int32)[None, :]).astype(jnp.int32)
    csum = jnp.cumsum(onehot, axis=0)
    rank = jnp.sum(onehot * (csum - 1), axis=1)
    counts = csum[-1]
    padded = (counts + tm - 1) // tm * tm
    pend = jnp.cumsum(padded)
    pstart = pend - padded
    dest = (pstart[e_flat] + rank).astype(jnp.int32)
    p = a + N_EXPERTS * tm
    nblk = p // tm
    src = jnp.zeros((p,), jnp.int32).at[dest].set(jnp.arange(a, dtype=jnp.int32) // TOP_K)
    blk_e = jnp.minimum(jnp.searchsorted(pend, jnp.arange(nblk, dtype=jnp.int32) * tm, side="right"),
                        N_EXPERTS - 1).astype(jnp.int32)
    n_used = (pend[-1:] // tm).astype(jnp.int32)
    pos = dest.reshape(n, TOP_K)
    return src, blk_e, n_used, pos[:, 0], pos[:, 1]


def _in_proj_tail(w_in, layer):
    w_in_l = w_in[layer, :, 2 * A_HEADS * A_QK + 2 * A_HEADS * A_V:]
    o = 0
    a_i = w_in_l[:, o:o + A_HEADS]
    a_f = w_in_l[:, o + A_HEADS:o + 2 * A_HEADS]
    o += 2 * A_HEADS
    c_q = w_in_l[:, o:o + Q_LORA]
    o += Q_LORA
    c_kv = w_in_l[:, o:o + KV_LORA]
    o += KV_LORA
    k_r = w_in_l[:, o:o + B_ROPE]
    o += B_ROPE
    d = w_in_l.shape[0]
    g_a = w_in_l[:, o:o + d]
    g_b = w_in_l[:, o + d:o + 2 * d]
    half = B_ROPE // 2
    k_r_sw = jnp.concatenate([-k_r[:, half:], k_r[:, :half]], axis=1)
    used = 2 * d + Q_LORA + KV_LORA + 2 * B_ROPE + 2 * A_HEADS
    pad = jnp.zeros((d, -used % 512), w_in_l.dtype)
    tail = jnp.concatenate([g_a, g_b, c_q, c_kv, k_r, k_r_sw, a_i, a_f, pad], axis=1)
    offs = dict(c_q=0, c_kv=Q_LORA, k_r=Q_LORA + KV_LORA, gates=Q_LORA + KV_LORA + LANES)
    return tail.astype(BF16), offs


def _uq_weight(w_uq_l):
    w = w_uq_l.reshape(Q_LORA, B_HEADS, B_NOPE + B_ROPE)
    nope, rope = w[..., :B_NOPE], w[..., B_NOPE:]
    half = B_ROPE // 2
    rope_sw = jnp.concatenate([-rope[..., half:], rope[..., :half]], axis=-1)
    return jnp.transpose(jnp.concatenate([nope, rope, rope_sw], axis=-1), (1, 0, 2)).astype(BF16)


def _ukv_weight(w_ukv_l):
    w = w_ukv_l.reshape(KV_LORA, B_HEADS, B_NOPE + B_V)
    return jnp.transpose(w, (1, 0, 2)).astype(BF16)


def _mixer(x, cs, batch, seq, layer, norm_w, w_in, b_i, b_f, conv_w, conv_b, mlstm_norm, w_a,
           q_norm, kv_norm, w_uq_l, w_ukv_l, w_b, w_out):
    d = x.shape[1]
    h = _rmsnorm(x, norm_w, BF16)
    main_w = 2 * A_HEADS * A_QK + 2 * A_HEADS * A_V
    z_main = _matmul_wcast(h, w_in, layer, main_w, F32, tm=1024, tn=512, name="in_proj_main")
    w_tail, offs = _in_proj_tail(w_in, layer)
    gates = _matmul(h, w_tail, 2 * d, BF16, tm=1024, tn=512, name="in_proj_gates")
    z_rest = _matmul(h, w_tail, w_tail.shape[1] - 2 * d, F32, tm=1024, tn=512, col0=2 * d, name="in_proj_rest")

    gbias = jnp.zeros((1, LANES), F32).at[0, :A_HEADS].set(b_i).at[0, A_HEADS:2 * A_HEADS].set(b_f)
    h_a = _mlstm(z_main, z_rest, offs["gates"] // LANES, conv_w, conv_b, gbias, mlstm_norm, batch, seq)

    cqn, ckvn, kr = _mla_prep(z_rest, cs, q_norm, kv_norm, offs["c_q"], offs["c_kv"], offs["k_r"])
    q = _uq_proj(cqn, _uq_weight(w_uq_l), cs)
    k, v = _ukv_proj(ckvn, _ukv_weight(w_ukv_l), kr)
    att = _attention(q, k, v, batch, seq)
    merged = _branches_merge(h_a, w_a.astype(BF16), att, w_b.astype(BF16), gates, tm=1024, tn=512)
    return _matmul_wcast(merged, w_out, layer, d, F32, tm=1024, tn=512, residual=x, name="out_proj")


def kernel(x, positions, norm_mix, w_in, b_igate, b_fgate, conv_w, conv_b, mlstm_norm, w_branch_a, q_norm, kv_norm, w_uq, w_ukv, w_branch_b, w_out, norm_ffn, w_gate_d, w_up_d, w_down_d, w_router, w_gate_e, w_up_e, w_down_e, norm_final):
    batch, seq, d = x.shape
    n = batch * seq
    depth = norm_mix.shape[0]
    x = x.reshape(n, d)
    cs = _rope_table(positions)
    wg_d, wu_d, wd_d = _cast_bf16(w_gate_d), _cast_bf16(w_up_d), _cast_bf16(w_down_d)
    wg_e, wu_e, wd_e = _cast_bf16(w_gate_e), _cast_bf16(w_up_e), _cast_bf16(w_down_e)
    out = None
    for l in range(depth):
        x = _mixer(x, cs, batch, seq, l, norm_mix[l], w_in, b_igate[l], b_fgate[l], conv_w[l], conv_b[l],
                   mlstm_norm[l], w_branch_a[l], q_norm[l], kv_norm[l], w_uq[l], w_ukv[l],
                   w_branch_b[l], w_out)
        last = l == depth - 1
        if l % 2 == 0:
            h = _rmsnorm(x, norm_ffn[l], BF16)
            tm = 512
            blk_e = jnp.zeros((n // tm,), jnp.int32)
            n_used = jnp.full((1,), n // tm, jnp.int32)
            e = l // 2
            x = _ffn(h, wg_d[e:e + 1], wu_d[e:e + 1], wd_d[e:e + 1], blk_e, n_used, x,
                     tm=tm, tf=512, x_buffers=2)
            if last:
                out = _rmsnorm(x, norm_final, F32)
        else:
            tm = 512
            h = _rmsnorm(x, norm_ffn[l], F32)
            idx, gates = _router(h, w_router[l // 2])
            src, blk_e, n_used, pos0, pos1 = _moe_plan(idx, tm)
            xs = _gather_rows(h, src)
            ys = _ffn(xs, wg_e[l // 2], wu_e[l // 2], wd_e[l // 2], blk_e, n_used, None,
                      tm=tm, tf=512, x_buffers=1)
            x = _combine(ys, x, gates, pos0, pos1, norm_final, normalize=last)
            if last:
                out = x
    return out.reshape(batch, seq, d)
```
